```python
import math
import jax, jax.numpy as jnp
from jax import lax
import numpy as np

D_MODEL = 1024
BATCH = 4
SEQ = 8192
DEPTH = 1

N_META = 16
BLOCK_Q = 128
EPS = 1e-6

MLA_HEADS = 8
MLA_Q_RANK = 256
MLA_KV_RANK = 256
MLA_NOPE_DIM = 64
MLA_ROPE_DIM = 32
MLA_V_DIM = 64
MLA_QK_DIM = MLA_NOPE_DIM + MLA_ROPE_DIM
ROPE_THETA = 10000.0

DIFF_HEADS = 4
DIFF_HEAD_DIM = 64
DIFF_V_DIM = 2 * DIFF_HEAD_DIM
DIFF_QK_WIDTH = DIFF_HEADS * 2 * DIFF_HEAD_DIM
DIFF_V_WIDTH = DIFF_HEADS * DIFF_V_DIM

NUM_BUCKETS = 32
MAX_DISTANCE = 128

_S1 = MLA_Q_RANK
_S2 = _S1 + MLA_KV_RANK
_S3 = _S2 + MLA_ROPE_DIM
_S4 = _S3 + DIFF_QK_WIDTH
_S5 = _S4 + DIFF_QK_WIDTH
IN_WIDTH = _S5 + DIFF_V_WIDTH
IN_SPLITS = (_S1, _S2, _S3, _S4, _S5)
MIX_WIDTH = MLA_HEADS * MLA_V_DIM + DIFF_V_WIDTH

PEER_HEADS = 8
PEER_N_KEYS = 128
PEER_N_EXPERTS = PEER_N_KEYS * PEER_N_KEYS
PEER_TOPK = 16
PEER_QUERY_DIM = 256
PEER_SUBKEY_DIM = PEER_QUERY_DIM // 2
PEER_CHUNK = 128

kernel_name = "hymba_mla_diffattn_peer_layer"


def _rms_norm(x, gain):
    xf = x.astype(jnp.float32)
    y = xf * lax.rsqrt(jnp.mean(xf * xf, axis=-1, keepdims=True) + EPS)
    return (y * gain.astype(jnp.float32)).astype(x.dtype)


def _rope(t, pos):
    half = t.shape[-1] // 2
    inv_freq = ROPE_THETA ** (-jnp.arange(half, dtype=jnp.float32) / half)
    ang = pos.astype(jnp.float32)[:, None] * inv_freq[None, :]
    cos = jnp.cos(ang)[:, None, :].astype(t.dtype)
    sin = jnp.sin(ang)[:, None, :].astype(t.dtype)
    t1, t2 = t[..., :half], t[..., half:]
    return jnp.concatenate([t1 * cos - t2 * sin, t1 * sin + t2 * cos], axis=-1)


def _t5_bucket(dist):
    n = jnp.maximum(dist, 0)
    max_exact = NUM_BUCKETS // 2
    nf = jnp.maximum(n, 1).astype(jnp.float32)
    large = max_exact + (jnp.log(nf / max_exact) / math.log(MAX_DISTANCE / max_exact)
                         * (NUM_BUCKETS - max_exact)).astype(jnp.int32)
    large = jnp.minimum(large, NUM_BUCKETS - 1)
    return jnp.where(n < max_exact, n, large)


def _causal_softmax(logits, q_pos, k_pos):
    mask = k_pos[None, :] <= q_pos[:, None]
    logits = jnp.where(mask, logits.astype(jnp.float32), -1e30)
    return jax.nn.softmax(logits, axis=-1)


def _to_blocks(t):
    b, l = t.shape[0], t.shape[1]
    t = t.reshape((b, l // BLOCK_Q, BLOCK_Q) + t.shape[2:])
    return jnp.moveaxis(t, 1, 0)


def _from_blocks(t):
    t = jnp.moveaxis(t, 0, 1)
    return t.reshape((t.shape[0], t.shape[1] * t.shape[2]) + t.shape[3:])


def _mla(c_q, c_kv, k_rope, q_norm, w_uq, kv_norm, w_ukv, qk_norm_q, qk_norm_k, pos):
    b, l, _ = c_q.shape
    q = (_rms_norm(c_q, q_norm) @ w_uq).reshape(b, l, MLA_HEADS, MLA_QK_DIM)
    kv = (_rms_norm(c_kv, kv_norm) @ w_ukv).reshape(b, l, MLA_HEADS, MLA_NOPE_DIM + MLA_V_DIM)
    k_nope, v = kv[..., :MLA_NOPE_DIM], kv[..., MLA_NOPE_DIM:]
    k_r = jnp.broadcast_to(k_rope[:, :, None, :], (b, l, MLA_HEADS, MLA_ROPE_DIM))
    k = jnp.concatenate([k_nope, k_r], axis=-1)
    q = _rms_norm(q, qk_norm_q)
    k = _rms_norm(k, qk_norm_k)
    q = jnp.concatenate([q[..., :MLA_NOPE_DIM], _rope(q[..., MLA_NOPE_DIM:], pos)], axis=-1)
    k = jnp.concatenate([k[..., :MLA_NOPE_DIM], _rope(k[..., MLA_NOPE_DIM:], pos)], axis=-1)
    scale = MLA_QK_DIM ** -0.5

    def block(args):
        q_start, qb = args
        q_pos = q_start + jnp.arange(BLOCK_Q, dtype=jnp.int32)
        s = jnp.einsum('bqhd,bkhd->bhqk', qb, k) * scale
        p = _causal_softmax(s, q_pos, pos)
        return jnp.einsum('bhqk,bkhd->bqhd', p.astype(v.dtype), v)

    starts = jnp.arange(l // BLOCK_Q, dtype=jnp.int32) * BLOCK_Q
    out = _from_blocks(lax.map(block, (starts, _to_blocks(q))))
    return out.reshape(b, l, MLA_HEADS * MLA_V_DIM)


def _diff_attn(q, k, v, q_norm, k_norm, lam_q1, lam_k1, lam_q2, lam_k2, subln, rel_bias, pos, lambda_init):
    b, l, _ = q.shape
    q = _rms_norm(q.reshape(b, l, DIFF_HEADS, 2, DIFF_HEAD_DIM), q_norm)
    k = _rms_norm(k.reshape(b, l, DIFF_HEADS, 2, DIFF_HEAD_DIM), k_norm)
    v = v.reshape(b, l, DIFF_HEADS, DIFF_V_DIM)
    lam = (jnp.exp(jnp.sum(lam_q1.astype(jnp.float32) * lam_k1.astype(jnp.float32)))
           - jnp.exp(jnp.sum(lam_q2.astype(jnp.float32) * lam_k2.astype(jnp.float32)))
           + lambda_init)
    k1, k2 = k[..., 0, :], k[..., 1, :]
    bias_table = rel_bias.astype(jnp.float32)
    scale = DIFF_HEAD_DIM ** -0.5

    def block(args):
        q_start, qb = args
        q_pos = q_start + jnp.arange(BLOCK_Q, dtype=jnp.int32)
        bias = bias_table[_t5_bucket(q_pos[:, None] - pos[None, :])]
        bias = jnp.transpose(bias, (2, 0, 1))[None]
        s1 = jnp.einsum('bqhd,bkhd->bhqk', qb[..., 0, :], k1).astype(jnp.float32) * scale + bias
        s2 = jnp.einsum('bqhd,bkhd->bhqk', qb[..., 1, :], k2).astype(jnp.float32) * scale + bias
        a = _causal_softmax(s1, q_pos, pos) - lam * _causal_softmax(s2, q_pos, pos)
        return jnp.einsum('bhqk,bkhd->bqhd', a.astype(v.dtype), v)

    starts = jnp.arange(l // BLOCK_Q, dtype=jnp.int32) * BLOCK_Q
    out = _from_blocks(lax.map(block, (starts, _to_blocks(q))))
    out = _rms_norm(out, subln) * (1.0 - lambda_init)
    return out.reshape(b, l, DIFF_V_WIDTH)


def _peer(h, w_query, sub_keys, u, v):
    b, l, d = h.shape
    tokens = h.reshape(b * l // PEER_CHUNK, PEER_CHUNK, d)

    def chunk(xc):
        q = (xc @ w_query).reshape(PEER_CHUNK, PEER_HEADS, 2, PEER_SUBKEY_DIM)
        s = jnp.einsum('chpd,hpnd->chpn', q, sub_keys).astype(jnp.float32)
        s1, i1 = lax.top_k(s[:, :, 0], PEER_TOPK)
        s2, i2 = lax.top_k(s[:, :, 1], PEER_TOPK)
        cand = (s1[..., :, None] + s2[..., None, :]).reshape(PEER_CHUNK, PEER_HEADS, PEER_TOPK * PEER_TOPK)
        best, flat = lax.top_k(cand, PEER_TOPK)
        e1 = jnp.take_along_axis(i1, flat // PEER_TOPK, axis=-1)
        e2 = jnp.take_along_axis(i2, flat % PEER_TOPK, axis=-1)
        experts = e1 * PEER_N_KEYS + e2
        g = jax.nn.softmax(best, axis=-1)
        u_sel = jnp.take(u, experts, axis=0)
        act = jax.nn.gelu(jnp.einsum('cd,chkd->chk', xc, u_sel).astype(jnp.float32), approximate=False)
        v_sel = jnp.take(v, experts, axis=0)
        return jnp.einsum('chk,chkd->cd', (g * act).astype(v.dtype), v_sel)

    return lax.map(chunk, tokens).reshape(b, l, d)


def setup_inputs(seed: int = 0) -> dict:
    key = jax.random.key(seed)
    ks = jax.random.split(key, 24)
    f32 = jnp.float32

    def nrm(k, shape, scale):
        return jax.random.normal(k, shape, f32) * scale

    def gain(k, shape):
        return 1.0 + 0.02 * jax.random.normal(k, shape, f32)

    return {
        "x": nrm(ks[0], (BATCH, SEQ, D_MODEL), 1.0),
        "meta_tokens": nrm(ks[1], (N_META, D_MODEL), 1.0),
        "rel_bias": nrm(ks[2], (NUM_BUCKETS, DIFF_HEADS), 0.5),
        "attn_norm": gain(ks[3], (DEPTH, D_MODEL)),
        "w_in": nrm(ks[4], (DEPTH, D_MODEL, IN_WIDTH), D_MODEL ** -0.5),
        "mla_q_norm": gain(ks[5], (DEPTH, MLA_Q_RANK)),
        "mla_w_uq": nrm(ks[6], (DEPTH, MLA_Q_RANK, MLA_HEADS * MLA_QK_DIM), MLA_Q_RANK ** -0.5),
        "mla_kv_norm": gain(ks[7], (DEPTH, MLA_KV_RANK)),
        "mla_w_ukv": nrm(ks[8], (DEPTH, MLA_KV_RANK, MLA_HEADS * (MLA_NOPE_DIM + MLA_V_DIM)), MLA_KV_RANK ** -0.5),
        "mla_qk_norm_q": gain(ks[9], (DEPTH, MLA_QK_DIM)),
        "mla_qk_norm_k": gain(ks[10], (DEPTH, MLA_QK_DIM)),
        "diff_q_norm": gain(ks[11], (DEPTH, DIFF_HEAD_DIM)),
        "diff_k_norm": gain(ks[12], (DEPTH, DIFF_HEAD_DIM)),
        "diff_lambda_q1": nrm(ks[13], (DEPTH, DIFF_HEAD_DIM), 0.1),
        "diff_lambda_k1": nrm(ks[14], (DEPTH, DIFF_HEAD_DIM), 0.1),
        "diff_lambda_q2": nrm(ks[15], (DEPTH, DIFF_HEAD_DIM), 0.1),
        "diff_lambda_k2": nrm(ks[16], (DEPTH, DIFF_HEAD_DIM), 0.1),
        "diff_subln": gain(ks[17], (DEPTH, DIFF_V_DIM)),
        "w_out": nrm(ks[18], (DEPTH, MIX_WIDTH, D_MODEL), MIX_WIDTH ** -0.5),
        "ffn_norm": gain(ks[19], (DEPTH, D_MODEL)),
        "peer_w_query": nrm(ks[20], (DEPTH, D_MODEL, PEER_HEADS * PEER_QUERY_DIM), D_MODEL ** -0.5),
        "peer_sub_keys": nrm(ks[21], (DEPTH, PEER_HEADS, 2, PEER_N_KEYS, PEER_SUBKEY_DIM), PEER_SUBKEY_DIM ** -0.5),
        "peer_u": nrm(ks[22], (DEPTH, PEER_N_EXPERTS, D_MODEL), D_MODEL ** -0.5),
        "peer_v": nrm(ks[23], (DEPTH, PEER_N_EXPERTS, D_MODEL), PEER_HEADS ** -0.5),
    }


def reference(x, meta_tokens, rel_bias, attn_norm, w_in, mla_q_norm, mla_w_uq, mla_kv_norm, mla_w_ukv,
              mla_qk_norm_q, mla_qk_norm_k, diff_q_norm, diff_k_norm, diff_lambda_q1, diff_lambda_k1,
              diff_lambda_q2, diff_lambda_k2, diff_subln, w_out, ffn_norm, peer_w_query, peer_sub_keys,
              peer_u, peer_v):
    b, s, d = x.shape
    l_real = N_META + s
    l_pad = -(-l_real // BLOCK_Q) * BLOCK_Q
    meta = jnp.broadcast_to(meta_tokens[None].astype(x.dtype), (b, N_META, d))
    h = jnp.concatenate([meta, x, jnp.zeros((b, l_pad - l_real, d), x.dtype)], axis=1)
    pos = jnp.arange(l_pad, dtype=jnp.int32)
    for layer in range(DEPTH):
        lambda_init = 0.8 - 0.6 * math.exp(-0.3 * layer)
        n = _rms_norm(h, attn_norm[layer])
        proj = n @ w_in[layer]
        c_q, c_kv, k_rope, dq, dk, dv = jnp.split(proj, IN_SPLITS, axis=-1)
        y_mla = _mla(c_q, c_kv, k_rope, mla_q_norm[layer], mla_w_uq[layer], mla_kv_norm[layer],
                     mla_w_ukv[layer], mla_qk_norm_q[layer], mla_qk_norm_k[layer], pos)
        y_diff = _diff_attn(dq, dk, dv, diff_q_norm[layer], diff_k_norm[layer], diff_lambda_q1[layer],
                            diff_lambda_k1[layer], diff_lambda_q2[layer], diff_lambda_k2[layer],
                            diff_subln[layer], rel_bias, pos, lambda_init)
        h = h + jnp.concatenate([y_mla, y_diff], axis=-1) @ w_out[layer]
        h = h + _peer(_rms_norm(h, ffn_norm[layer]), peer_w_query[layer], peer_sub_keys[layer],
                      peer_u[layer], peer_v[layer])
    return h[:, N_META:l_real]
```

```python
import functools
import math

import numpy as np
import jax
import jax.numpy as jnp
from jax import lax
from jax.experimental import pallas as pl
from jax.experimental.pallas import tpu as pltpu

F32 = jnp.float32
BF16 = jnp.bfloat16

N_META = 16
EPS = 1e-6
LANES = 128

MLA_HEADS = 8
MLA_Q_RANK = 256
MLA_KV_RANK = 256
MLA_NOPE = 64
MLA_ROPE = 32
MLA_V = 64
MLA_QK = MLA_NOPE + MLA_ROPE
ROPE_THETA = 10000.0

DIFF_HEADS = 4
DIFF_HD = 64
DIFF_V = 2 * DIFF_HD

NUM_BUCKETS = 32
MAX_DISTANCE = 128

PEER_HEADS = 8
PEER_KEYS = 128
PEER_TOPK = 16
PEER_SUB = 128

NEG = -1e30

PROJ_ROWS = 256
ATTN_TILE = 256
FRONT_ROWS = 256
DENSE_TOKENS = 512
DENSE_EXPERTS = 1024
VMEM_LIMIT = 56 * 1024 * 1024


def _rms(x, g):
    return x * lax.rsqrt(jnp.mean(x * x, axis=-1, keepdims=True) + EPS) * g


def _dot(a, b):
    return jnp.dot(a, b, preferred_element_type=F32)


def _dot_nt(a, b):
    return lax.dot_general(a, b, (((1,), (1,)), ((), ())), preferred_element_type=F32)


def _proj_kernel(x_ref, g1_ref, win_ref, gcq_ref, wuq_ref, gckv_ref, wkv_ref, gq_ref, gk_ref,
                 gdq_ref, gdk_ref, cos_ref, sina_ref, sinb_ref,
                 qm_ref, km_ref, vm_ref, qd_ref, kd_ref, vd_ref, *, scale_m, scale_d):
    x = x_ref[0]
    n = _rms(x, g1_ref[...])
    proj = _dot(n.astype(BF16), win_ref[...])
    cq = proj[:, 0:256]
    ckv = proj[:, 256:512]
    krp = proj[:, 512:640]
    dq = proj[:, 640:1152]
    dk = proj[:, 1152:1664]
    dv = proj[:, 1664:2176]
    q_raw = _dot(_rms(cq, gcq_ref[...]).astype(BF16), wuq_ref[...])
    kv = _dot(_rms(ckv, gckv_ref[...]).astype(BF16), wkv_ref[...])
    cos = cos_ref[...]
    sina = sina_ref[...]
    sinb = sinb_ref[...]
    gq = gq_ref[...]
    gk = gk_ref[...]

    def rope(t):
        return t * cos + pltpu.roll(t, LANES - 16, 1) * sina + pltpu.roll(t, 16, 1) * sinb

    inv_qk = 1.0 / MLA_QK
    for h in range(MLA_HEADS):
        sl = slice(h * LANES, (h + 1) * LANES)
        qh = q_raw[:, sl]
        rq = lax.rsqrt(jnp.sum(qh * qh, axis=-1, keepdims=True) * inv_qk + EPS)
        qm_ref[0, h] = (rope(qh * rq * gq) * scale_m).astype(BF16)
        kh = kv[:, sl] + krp
        rk = lax.rsqrt(jnp.sum(kh * kh, axis=-1, keepdims=True) * inv_qk + EPS)
        km_ref[0, h] = rope(kh * rk * gk).astype(BF16)
    for hp in range(MLA_HEADS // 2):
        vm_ref[0, hp] = kv[:, 1024 + hp * LANES:1024 + (hp + 1) * LANES].astype(BF16)

    lo = lax.broadcasted_iota(jnp.int32, (1, LANES), 1) < DIFF_HD
    inv_hd = 1.0 / DIFF_HD

    def halfnorm(t, g):
        sq = t * t
        s_lo = jnp.sum(jnp.where(lo, sq, 0.0), axis=-1, keepdims=True)
        s_hi = jnp.sum(jnp.where(lo, 0.0, sq), axis=-1, keepdims=True)
        r = jnp.where(lo, lax.rsqrt(s_lo * inv_hd + EPS), lax.rsqrt(s_hi * inv_hd + EPS))
        return t * r * g

    gdq = gdq_ref[...]
    gdk = gdk_ref[...]
    for h in range(DIFF_HEADS):
        sl = slice(h * LANES, (h + 1) * LANES)
        qn = halfnorm(dq[:, sl], gdq) * scale_d
        qd_ref[0, h, 0] = jnp.where(lo, qn, 0.0).astype(BF16)
        qd_ref[0, h, 1] = jnp.where(lo, 0.0, qn).astype(BF16)
        kd_ref[0, h] = halfnorm(dk[:, sl], gdk).astype(BF16)
        vd_ref[0, h] = dv[:, sl].astype(BF16)


def _project(x3, tabs, wts, rows):
    b, s, d = x3.shape
    rows = min(rows, s)
    assert s % rows == 0
    grid = (b, s // rows)
    full = lambda a: pl.BlockSpec(a.shape, lambda i, j: (0,) * a.ndim)
    tab_spec = pl.BlockSpec((rows, LANES), lambda i, j: (j, 0))
    in_specs = [pl.BlockSpec((1, rows, d), lambda i, j: (i, j, 0))] + [full(w) for w in wts] + [tab_spec] * 3
    hs = lambda nh: pl.BlockSpec((1, nh, rows, LANES), lambda i, j: (i, 0, j, 0))
    out_shape = (
        jax.ShapeDtypeStruct((b, MLA_HEADS, s, LANES), BF16),
        jax.ShapeDtypeStruct((b, MLA_HEADS, s, LANES), BF16),
        jax.ShapeDtypeStruct((b, MLA_HEADS // 2, s, LANES), BF16),
        jax.ShapeDtypeStruct((b, DIFF_HEADS, 2, s, LANES), BF16),
        jax.ShapeDtypeStruct((b, DIFF_HEADS, s, LANES), BF16),
        jax.ShapeDtypeStruct((b, DIFF_HEADS, s, LANES), BF16),
    )
    out_specs = (hs(MLA_HEADS), hs(MLA_HEADS), hs(MLA_HEADS // 2),
                 pl.BlockSpec((1, DIFF_HEADS, 2, rows, LANES), lambda i, j: (i, 0, 0, j, 0)),
                 hs(DIFF_HEADS), hs(DIFF_HEADS))
    kern = functools.partial(_proj_kernel, scale_m=MLA_QK ** -0.5, scale_d=DIFF_HD ** -0.5)
    return pl.pallas_call(
        kern, grid=grid, in_specs=in_specs, out_specs=out_specs, out_shape=out_shape,
        compiler_params=pltpu.CompilerParams(dimension_semantics=("arbitrary", "arbitrary"),
                                             vmem_limit_bytes=VMEM_LIMIT),
        name="proj",
    )(x3, *wts, *tabs)


def _flash_kernel(*refs, hp, nm, tq, diff, lambda_init):
    if diff:
        (q_ref, k_ref, v_ref, kmeta_ref, vmeta_ref, bmeta_ref, bdiag_ref, bprev_ref,
         lq1_ref, lk1_ref, lq2_ref, lk2_ref, subln_ref, o_ref) = refs
    else:
        q_ref, k_ref, v_ref, kmeta_ref, vmeta_ref, bmeta_ref, bdiag_ref, o_ref = refs
        bprev_ref = None
    qi = pl.program_id(2)
    rows = nm * tq
    outs = []
    for hh in range(hp):
        vh = 0 if not diff else hh
        q = q_ref[0, hh].reshape(rows, LANES)

        s = _dot_nt(q, kmeta_ref[hh]) + bmeta_ref[0, 0]
        m0 = jnp.max(s, axis=-1, keepdims=True)
        p = jnp.exp(s - m0)
        l0 = jnp.sum(p, axis=-1, keepdims=True)
        acc0 = _dot(p.astype(BF16), vmeta_ref[vh])

        def step(j, carry, bias_ref=None, hh=hh, vh=vh, q=q):
            m, l, acc = carry
            off = pl.multiple_of(j * tq, tq)
            s = _dot_nt(q, k_ref[0, hh, pl.ds(off, tq), :])
            if bias_ref is not None:
                s = s + bias_ref[0]
            m_new = jnp.maximum(m, jnp.max(s, axis=-1, keepdims=True))
            alpha = jnp.exp(m - m_new)
            p = jnp.exp(s - m_new)
            l = alpha * l + jnp.sum(p, axis=-1, keepdims=True)
            acc = alpha * acc + _dot(p.astype(BF16), v_ref[0, vh, pl.ds(off, tq), :])
            return m_new, l, acc

        carry = (m0, l0, acc0)
        if diff:
            carry = lax.fori_loop(0, jnp.maximum(qi - 1, 0), step, carry)
            carry = lax.cond(qi > 0,
                             lambda c: step(qi - 1, c, bprev_ref),
                             lambda c: c, carry)
        else:
            carry = lax.fori_loop(0, qi, step, carry)
        m, l, acc = step(qi, carry, bdiag_ref)
        outs.append(acc / l)

    if diff:
        lam = (jnp.exp(jnp.sum(lq1_ref[...] * lk1_ref[...], axis=-1, keepdims=True))
               - jnp.exp(jnp.sum(lq2_ref[...] * lk2_ref[...], axis=-1, keepdims=True)) + lambda_init)
        o = outs[0]
        d = o[:tq] - lam * o[tq:]
        y = _rms(d, subln_ref[...]) * (1.0 - lambda_init)
        o_ref[0] = y.astype(BF16)
    else:
        lo = lax.broadcasted_iota(jnp.int32, (1, LANES), 1) < MLA_V
        o_ref[0] = jnp.where(lo, outs[0], outs[1]).astype(BF16)


def _flash(q, k, v, kmeta, vmeta, bmeta, bdiag, bprev, extra, *, diff, tq, lambda_init):
    b, g, nm, s, _ = q.shape
    hp = 1 if diff else 2
    ngrid = g // hp
    nq = s // tq
    rows = nm * tq
    grid = (b, ngrid, nq)
    gsel = (lambda gi: gi) if diff else (lambda gi: 0)
    in_specs = [
        pl.BlockSpec((1, hp, nm, tq, LANES), lambda bi, gi, qi: (bi, gi, 0, qi, 0)),
        pl.BlockSpec((1, hp, s, LANES), lambda bi, gi, qi: (bi, gi, 0, 0)),
        pl.BlockSpec((1, 1, s, LANES), lambda bi, gi, qi: (bi, gi, 0, 0)),
        pl.BlockSpec((hp, LANES, LANES), lambda bi, gi, qi: (gi, 0, 0)),
        pl.BlockSpec((1, LANES, LANES), lambda bi, gi, qi: (gi, 0, 0)),
        pl.BlockSpec((1, 1, rows, LANES), lambda bi, gi, qi: (jnp.minimum(qi, 1), gsel(gi), 0, 0)),
        pl.BlockSpec((1, rows, tq), lambda bi, gi, qi: (gsel(gi), 0, 0)),
    ]
    args = [q, k, v, kmeta, vmeta, bmeta, bdiag]
    if diff:
        in_specs.append(pl.BlockSpec((1, rows, tq), lambda bi, gi, qi: (gi, 0, 0)))
        args.append(bprev)
        for e in extra:
            in_specs.append(pl.BlockSpec(e.shape, lambda bi, gi, qi: (0, 0)))
            args.append(e)
    kern = functools.partial(_flash_kernel, hp=hp, nm=nm, tq=tq, diff=diff, lambda_init=lambda_init)
    return pl.pallas_call(
        kern, grid=grid, in_specs=in_specs,
        out_specs=pl.BlockSpec((1, tq, LANES), lambda bi, gi, qi: (bi, qi, gi)),
        out_shape=jax.ShapeDtypeStruct((b, s, ngrid * LANES), BF16),
        compiler_params=pltpu.CompilerParams(dimension_semantics=("arbitrary",) * 3,
                                             vmem_limit_bytes=VMEM_LIMIT),
        name="flash_diff" if diff else "flash_mla",
    )(*args)


def _top_values(s, n, row_ref=None):
    vals = []
    for i in range(n):
        m = jnp.max(s, axis=0, keepdims=True)
        vals.append(m)
        if row_ref is not None and i < PEER_TOPK:
            row_ref[i:i + 1, :] = m
        if i + 1 < n:
            s = jnp.where(s == m, -jnp.inf, s)
    return vals


def _front_kernel(x_ref, ym_ref, yd_ref, woa_ref, wob_ref, g2_ref, wqt_ref, sk_ref,
                  h1_ref, n2t_ref, a_ref, p_ref, nb_ref, r_ref, v2s_ref, cand_ref):
    h1 = x_ref[...] + _dot(ym_ref[...], woa_ref[...]) + _dot(yd_ref[...], wob_ref[...])
    h1_ref[...] = h1
    n2t = _rms(h1, g2_ref[...]).T.astype(BF16)
    n2t_ref[...] = n2t
    qpt = _dot(wqt_ref[...], n2t).astype(BF16)
    k = PEER_TOPK
    for h in range(PEER_HEADS):
        s1 = _dot(sk_ref[2 * h], qpt[(2 * h) * PEER_SUB:(2 * h + 1) * PEER_SUB])
        s2 = _dot(sk_ref[2 * h + 1], qpt[(2 * h + 1) * PEER_SUB:(2 * h + 2) * PEER_SUB])
        v1 = _top_values(s1, k + 1)
        v2 = _top_values(s2, k + 1, v2s_ref)
        v2s = v2s_ref[...]
        for i in range(k):
            cand_ref[i * k:(i + 1) * k, :] = v1[i] + v2s
        c = _top_values(cand_ref[...], k + 1)
        c17 = jnp.maximum(c[k], jnp.maximum(v1[k] + v2[0], v1[0] + v2[k]))
        tau = 0.5 * (c[k - 1] + c17)
        z = jnp.exp(c[0] - c[0])
        for i in range(1, k):
            z = z + jnp.exp(c[i] - c[0])
        a_ref[h] = s1 - tau
        nb_ref[h] = -s2
        p_ref[h] = jnp.exp(s1 - v1[0]) / z
        r_ref[h] = jnp.exp(s2 - v2[0])


def _front(x2, ym, yd, woa, wob, g2, wqt, sk, rows):
    t, d = x2.shape
    rows = min(rows, t)
    grid = (t // rows,)
    full = lambda a: pl.BlockSpec(a.shape, lambda i: (0,) * a.ndim)
    gate = jax.ShapeDtypeStruct((PEER_HEADS, PEER_KEYS, t), F32)
    gate_spec = pl.BlockSpec((PEER_HEADS, PEER_KEYS, rows), lambda i: (0, 0, i))
    return pl.pallas_call(
        _front_kernel, grid=grid,
        in_specs=[pl.BlockSpec((rows, d), lambda i: (i, 0)),
                  pl.BlockSpec((rows, ym.shape[1]), lambda i: (i, 0)),
                  pl.BlockSpec((rows, yd.shape[1]), lambda i: (i, 0)),
                  full(woa), full(wob), full(g2), full(wqt), full(sk)],
        out_specs=(pl.BlockSpec((rows, d), lambda i: (i, 0)),
                   pl.BlockSpec((d, rows), lambda i: (0, i)),
                   gate_spec, gate_spec, gate_spec, gate_spec),
        out_shape=(jax.ShapeDtypeStruct((t, d), F32), jax.ShapeDtypeStruct((d, t), BF16),
                   gate, gate, gate, gate),
        scratch_shapes=[pltpu.VMEM((PEER_TOPK, rows), F32),
                        pltpu.VMEM((PEER_TOPK * PEER_TOPK, rows), F32)],
        compiler_params=pltpu.CompilerParams(dimension_semantics=("arbitrary",),
                                             vmem_limit_bytes=VMEM_LIMIT),
        name="front",
    )(x2, ym, yd, woa, wob, g2, wqt, sk)


def _dense_kernel(n2t_ref, u_ref, vt_ref, a_ref, p_ref, nb_ref, r_ref, h1_ref, o_ref, acc_ref, w_ref):
    e = pl.program_id(1)

    @pl.when(e == 0)
    def _():
        acc_ref[...] = jnp.zeros_like(acc_ref)

    pre = _dot(u_ref[...], n2t_ref[...])
    act = 0.5 * pre * (1.0 + lax.erf(pre * math.sqrt(0.5)))
    n_i = u_ref.shape[0] // PEER_KEYS
    for ii in range(n_i):
        g = None
        for h in range(PEER_HEADS):
            sel = a_ref[h, ii:ii + 1, :] >= nb_ref[h]
            term = jnp.where(sel, p_ref[h, ii:ii + 1, :] * r_ref[h], 0.0)
            g = term if g is None else g + term
        w_ref[ii * PEER_KEYS:(ii + 1) * PEER_KEYS, :] = (
            g * act[ii * PEER_KEYS:(ii + 1) * PEER_KEYS]).astype(BF16)
    acc_ref[...] += _dot(vt_ref[...], w_ref[...])

    @pl.when(e == pl.num_programs(1) - 1)
    def _():
        o_ref[...] = h1_ref[...] + acc_ref[...].T


def _dense(n2t, u, vt, a, p, nb, r, h1, tokens, experts):
    d, t = n2t.shape
    ne = u.shape[0]
    tokens = min(tokens, t)
    grid = (t // tokens, ne // experts)
    n_i = experts // PEER_KEYS
    return pl.pallas_call(
        _dense_kernel, grid=grid,
        in_specs=[pl.BlockSpec((d, tokens), lambda ti, ei: (0, ti)),
                  pl.BlockSpec((experts, d), lambda ti, ei: (ei, 0)),
                  pl.BlockSpec((d, experts), lambda ti, ei: (0, ei)),
                  pl.BlockSpec((PEER_HEADS, n_i, tokens), lambda ti, ei: (0, ei, ti)),
                  pl.BlockSpec((PEER_HEADS, n_i, tokens), lambda ti, ei: (0, ei, ti)),
                  pl.BlockSpec((PEER_HEADS, PEER_KEYS, tokens), lambda ti, ei: (0, 0, ti)),
                  pl.BlockSpec((PEER_HEADS, PEER_KEYS, tokens), lambda ti, ei: (0, 0, ti)),
                  pl.BlockSpec((tokens, d), lambda ti, ei: (ti, 0))],
        out_specs=pl.BlockSpec((tokens, d), lambda ti, ei: (ti, 0)),
        out_shape=jax.ShapeDtypeStruct((t, d), F32),
        scratch_shapes=[pltpu.VMEM((d, tokens), F32), pltpu.VMEM((experts, tokens), BF16)],
        compiler_params=pltpu.CompilerParams(dimension_semantics=("arbitrary", "arbitrary"),
                                             vmem_limit_bytes=VMEM_LIMIT),
        name="dense",
    )(n2t, u, vt, a, p, nb, r, h1)


def _rope_tables(pos):
    half = MLA_ROPE // 2
    inv_freq = ROPE_THETA ** (-jnp.arange(half, dtype=F32) / half)
    ang = pos.astype(F32)[:, None] * inv_freq[None, :]
    cos, sin = jnp.cos(ang), jnp.sin(ang)
    n = pos.shape[0]
    one = jnp.ones((n, MLA_NOPE), F32)
    zero = jnp.zeros((n, MLA_NOPE), F32)
    tail1 = jnp.ones((n, LANES - MLA_QK), F32)
    tail0 = jnp.zeros((n, LANES - MLA_QK), F32)
    zh = jnp.zeros((n, half), F32)
    cos_t = jnp.concatenate([one, cos, cos, tail1], axis=1)
    sina = jnp.concatenate([zero, -sin, zh, tail0], axis=1)
    sinb = jnp.concatenate([zero, zh, sin, tail0], axis=1)
    return cos_t, sina, sinb


def _bucket_np(dist):
    n = np.maximum(dist, 0)
    max_exact = NUM_BUCKETS // 2
    nf = np.maximum(n, 1).astype(np.float32)
    large = max_exact + (np.log(nf / np.float32(max_exact)) / np.float32(math.log(MAX_DISTANCE / max_exact))
                         * np.float32(NUM_BUCKETS - max_exact)).astype(np.int32)
    large = np.minimum(large, NUM_BUCKETS - 1)
    return np.where(n < max_exact, n, large)


def _bias_tiles(rel_bias, tq, nm):
    far = int(np.min(np.nonzero(_bucket_np(np.arange(4 * MAX_DISTANCE)) == NUM_BUCKETS - 1)[0]))
    assert np.all(_bucket_np(np.arange(far, 8 * MAX_DISTANCE)) == NUM_BUCKETS - 1)
    table = rel_bias.astype(F32)
    shifted = table - table[NUM_BUCKETS - 1][None, :]

    def lookup(dist):
        vis = dist >= 0
        bk = _bucket_np(np.where(vis, dist, 0))
        vals = jnp.transpose(shifted[bk], (2, 0, 1))
        return jnp.where(vis[None], vals, NEG)

    r = np.arange(tq)[:, None]
    c = np.arange(tq)[None, :]
    diag = lookup(r - c)
    prev = lookup(r - c + tq)
    cm = np.arange(LANES)[None, :]
    dm = np.where(cm < N_META, N_META + r - cm, -1)
    meta0 = lookup(dm)
    meta_far = lookup(np.where(cm < N_META, N_META + r - cm + tq, -1))
    if tq < far:
        raise NotImplementedError("attention tile smaller than the relative-bias window")
    stack = lambda t: jnp.concatenate([t] * nm, axis=1)
    return jnp.stack([stack(meta0), stack(meta_far)]), stack(diag), stack(prev)


def _mask_tiles(tq):
    r = np.arange(tq)[:, None]
    c = np.arange(tq)[None, :]
    diag = np.where(r >= c, 0.0, NEG).astype(np.float32)[None]
    cm = np.arange(LANES)[None, :]
    meta = np.broadcast_to(np.where(cm < N_META, 0.0, NEG).astype(np.float32), (tq, LANES))
    return jnp.asarray(np.stack([meta, meta])[:, None]), jnp.asarray(diag)


def _pad_lanes(a, width):
    return jnp.pad(a, [(0, 0)] * (a.ndim - 1) + [(0, width - a.shape[-1])])


def kernel(x, meta_tokens, rel_bias, attn_norm, w_in, mla_q_norm, mla_w_uq, mla_kv_norm, mla_w_ukv,
           mla_qk_norm_q, mla_qk_norm_k, diff_q_norm, diff_k_norm, diff_lambda_q1, diff_lambda_k1,
           diff_lambda_q2, diff_lambda_k2, diff_subln, w_out, ffn_norm, peer_w_query, peer_sub_keys,
           peer_u, peer_v):
    b, s, d = x.shape
    assert attn_norm.shape[0] == 1, "meta rows are only used as keys: single layer"
    lambda_init = 0.8 - 0.6 * math.exp(-0.3 * 0)
    tq = min(ATTN_TILE, s)
    assert s % tq == 0

    wi = w_in[0]
    s1, s2, s3 = MLA_Q_RANK, MLA_Q_RANK + MLA_KV_RANK, MLA_Q_RANK + MLA_KV_RANK + MLA_ROPE
    wkr = jnp.pad(wi[:, s2:s3], ((0, 0), (MLA_NOPE, LANES - MLA_QK)))
    w_in_arr = jnp.concatenate([wi[:, :s2], wkr, wi[:, s3:]], axis=1).astype(BF16)
    wuq = _pad_lanes(mla_w_uq[0].reshape(MLA_Q_RANK, MLA_HEADS, MLA_QK), LANES).reshape(MLA_Q_RANK, -1).astype(BF16)
    wukv = mla_w_ukv[0].reshape(MLA_KV_RANK, MLA_HEADS, MLA_NOPE + MLA_V)
    wk = _pad_lanes(wukv[..., :MLA_NOPE], LANES).reshape(MLA_KV_RANK, -1)
    wv = wukv[..., MLA_NOPE:].reshape(MLA_KV_RANK, -1)
    wkv = jnp.concatenate([wk, wv], axis=1).astype(BF16)
    row = lambda a: a.reshape(1, -1).astype(F32)
    gq = _pad_lanes(row(mla_qk_norm_q[0]), LANES)
    gk = _pad_lanes(row(mla_qk_norm_k[0]), LANES)
    gdq = jnp.tile(row(diff_q_norm[0]), (1, 2))
    gdk = jnp.tile(row(diff_k_norm[0]), (1, 2))
    wts = (row(attn_norm[0]), w_in_arr, row(mla_q_norm[0]), wuq, row(mla_kv_norm[0]), wkv, gq, gk, gdq, gdk)

    pos_real = jnp.arange(N_META, N_META + s, dtype=jnp.int32)
    qm, km, vm, qd, kd, vd = _project(x, _rope_tables(pos_real), wts, PROJ_ROWS)
    pos_meta = jnp.arange(N_META, dtype=jnp.int32)
    _, km_m, vm_m, _, kd_m, vd_m = _project(meta_tokens[None].astype(x.dtype), _rope_tables(pos_meta), wts, N_META)
    padk = lambda a: jnp.pad(a[0], ((0, 0), (0, LANES - N_META), (0, 0)))

    mmeta, mdiag = _mask_tiles(tq)
    y_mla = _flash(qm[:, :, None], km, vm, padk(km_m), padk(vm_m), mmeta, mdiag, None, (),
                   diff=False, tq=tq, lambda_init=lambda_init)
    bmeta, bdiag, bprev = _bias_tiles(rel_bias, tq, 2)
    extra = (row(diff_lambda_q1[0]), row(diff_lambda_k1[0]), row(diff_lambda_q2[0]), row(diff_lambda_k2[0]),
             row(diff_subln[0]))
    y_diff = _flash(qd, kd, vd, padk(kd_m), padk(vd_m), bmeta, bdiag, bprev, extra,
                    diff=True, tq=tq, lambda_init=lambda_init)

    t = b * s
    wo = w_out[0].astype(BF16)
    n_mla = MLA_HEADS * MLA_V
    wqt = peer_w_query[0].T.astype(BF16)
    sk = peer_sub_keys[0].reshape(PEER_HEADS * 2, PEER_KEYS, PEER_SUB).astype(BF16)
    h1, n2t, ga, gp, gnb, gr = _front(x.reshape(t, d), y_mla.reshape(t, -1), y_diff.reshape(t, -1),
                                      wo[:n_mla], wo[n_mla:], row(ffn_norm[0]), wqt, sk, FRONT_ROWS)
    u_bf = peer_u[0].astype(BF16)
    vt_bf = peer_v[0].T.astype(BF16)
    out = _dense(n2t, u_bf, vt_bf, ga, gp, gnb, gr, h1, DENSE_TOKENS, DENSE_EXPERTS)
    return out.reshape(b, s, d)
```

```python
import functools
import math

import numpy as np
import jax
import jax.numpy as jnp
from jax import lax
from jax.experimental import pallas as pl
from jax.experimental.pallas import tpu as pltpu

F32 = jnp.float32
BF16 = jnp.bfloat16

N_META = 16
EPS = 1e-6
LANES = 128
SUBLANES = 8

MLA_HEADS = 8
MLA_Q_RANK = 256
MLA_KV_RANK = 256
MLA_NOPE = 64
MLA_ROPE = 32
MLA_V = 64
MLA_QK = MLA_NOPE + MLA_ROPE
ROPE_THETA = 10000.0

DIFF_HEADS = 4
DIFF_HD = 64
DIFF_V = 2 * DIFF_HD

NUM_BUCKETS = 32
MAX_DISTANCE = 128

PEER_HEADS = 8
PEER_KEYS = 128
PEER_TOPK = 16
PEER_SUB = 128

NEG = -1e30

PROJ_ROWS = 256
ATTN_TILE = 512
ATTN_CHAIN_ROWS = 256
LOG2E = math.log2(math.e)
FRONT_ROWS = 256
DENSE_TOKENS = 512
DENSE_EXPERTS = 1024
DENSE_PIECE = 256
GATE_ROWS = 32
VMEM_LIMIT = 56 * 1024 * 1024


def _rms(x, g):
    return x * lax.rsqrt(jnp.mean(x * x, axis=-1, keepdims=True) + EPS) * g


def _dot(a, b):
    return jnp.dot(a, b, preferred_element_type=F32)


def _dot_nt(a, b):
    return lax.dot_general(a, b, (((1,), (1,)), ((), ())), preferred_element_type=F32)


def _proj_kernel(x_ref, g1_ref, win_ref, gcq_ref, wuq_ref, gckv_ref, wkv_ref, gq_ref, gk_ref,
                 gdq_ref, gdk_ref, cos_ref, sina_ref, sinb_ref,
                 qm_ref, km_ref, vm_ref, qd_ref, kd_ref, vd_ref, *, scale_m, scale_d):
    x = x_ref[0]
    n = _rms(x, g1_ref[...])
    proj = _dot(n.astype(BF16), win_ref[...])
    cq = proj[:, 0:256]
    ckv = proj[:, 256:512]
    krp = proj[:, 512:640]
    dq = proj[:, 640:1152]
    dk = proj[:, 1152:1664]
    dv = proj[:, 1664:2176]
    q_raw = _dot(_rms(cq, gcq_ref[...]).astype(BF16), wuq_ref[...])
    kv = _dot(_rms(ckv, gckv_ref[...]).astype(BF16), wkv_ref[...])
    cos = cos_ref[...]
    sina = sina_ref[...]
    sinb = sinb_ref[...]
    gq = gq_ref[...]
    gk = gk_ref[...]

    def rope(t):
        return t * cos + pltpu.roll(t, LANES - 16, 1) * sina + pltpu.roll(t, 16, 1) * sinb

    inv_qk = 1.0 / MLA_QK
    for h in range(MLA_HEADS):
        sl = slice(h * LANES, (h + 1) * LANES)
        qh = q_raw[:, sl]
        rq = lax.rsqrt(jnp.sum(qh * qh, axis=-1, keepdims=True) * inv_qk + EPS)
        qm_ref[0, h] = (rope(qh * rq * gq) * scale_m).astype(BF16)
        kh = kv[:, sl] + krp
        rk = lax.rsqrt(jnp.sum(kh * kh, axis=-1, keepdims=True) * inv_qk + EPS)
        km_ref[0, h] = rope(kh * rk * gk).astype(BF16)
    lo_v = lax.broadcasted_iota(jnp.int32, (1, LANES), 1) < MLA_V
    for h in range(MLA_HEADS):
        vm_ref[0, h] = jnp.where(lo_v, kv[:, 1024 + h * LANES:1024 + (h + 1) * LANES], 1.0).astype(BF16)

    lo = lax.broadcasted_iota(jnp.int32, (1, LANES), 1) < DIFF_HD
    inv_hd = 1.0 / DIFF_HD

    def halfnorm(t, g):
        sq = t * t
        s_lo = jnp.sum(jnp.where(lo, sq, 0.0), axis=-1, keepdims=True)
        s_hi = jnp.sum(jnp.where(lo, 0.0, sq), axis=-1, keepdims=True)
        r = jnp.where(lo, lax.rsqrt(s_lo * inv_hd + EPS), lax.rsqrt(s_hi * inv_hd + EPS))
        return t * r * g

    gdq = gdq_ref[...]
    gdk = gdk_ref[...]
    for h in range(DIFF_HEADS):
        sl = slice(h * LANES, (h + 1) * LANES)
        qn = halfnorm(dq[:, sl], gdq) * scale_d
        qd_ref[0, h, 0] = jnp.where(lo, qn, 0.0).astype(BF16)
        qd_ref[0, h, 1] = jnp.where(lo, 0.0, qn).astype(BF16)
        kd_ref[0, h] = halfnorm(dk[:, sl], gdk).astype(BF16)
        vd_ref[0, h, :, 0:LANES] = dv[:, sl].astype(BF16)
        vd_ref[0, h, :, LANES:2 * LANES] = jnp.ones_like(dv[:, sl]).astype(BF16)


def _project(x3, tabs, wts, rows):
    b, s, d = x3.shape
    rows = min(rows, s)
    assert s % rows == 0
    grid = (b, s // rows)
    full = lambda a: pl.BlockSpec(a.shape, lambda i, j: (0,) * a.ndim)
    tab_spec = pl.BlockSpec((rows, LANES), lambda i, j: (j, 0))
    in_specs = [pl.BlockSpec((1, rows, d), lambda i, j: (i, j, 0))] + [full(w) for w in wts] + [tab_spec] * 3
    hs = lambda nh: pl.BlockSpec((1, nh, rows, LANES), lambda i, j: (i, 0, j, 0))
    out_shape = (
        jax.ShapeDtypeStruct((b, MLA_HEADS, s, LANES), BF16),
        jax.ShapeDtypeStruct((b, MLA_HEADS, s, LANES), BF16),
        jax.ShapeDtypeStruct((b, MLA_HEADS, s, LANES), BF16),
        jax.ShapeDtypeStruct((b, DIFF_HEADS, 2, s, LANES), BF16),
        jax.ShapeDtypeStruct((b, DIFF_HEADS, s, LANES), BF16),
        jax.ShapeDtypeStruct((b, DIFF_HEADS, s, 2 * LANES), BF16),
    )
    out_specs = (hs(MLA_HEADS), hs(MLA_HEADS), hs(MLA_HEADS),
                 pl.BlockSpec((1, DIFF_HEADS, 2, rows, LANES), lambda i, j: (i, 0, 0, j, 0)),
                 hs(DIFF_HEADS),
                 pl.BlockSpec((1, DIFF_HEADS, rows, 2 * LANES), lambda i, j: (i, 0, j, 0)))
    kern = functools.partial(_proj_kernel, scale_m=MLA_QK ** -0.5 * LOG2E, scale_d=DIFF_HD ** -0.5 * LOG2E)
    return pl.pallas_call(
        kern, grid=grid, in_specs=in_specs, out_specs=out_specs, out_shape=out_shape,
        compiler_params=pltpu.CompilerParams(dimension_semantics=("arbitrary", "arbitrary"),
                                             vmem_limit_bytes=VMEM_LIMIT),
        name="proj",
    )(x3, *wts, *tabs)


def _flash_kernel(*refs, hp, nm, tq, rc, diff, lambda_init):
    if diff:
        (q_ref, k_ref, v_ref, kmeta_ref, vmeta_ref, bmeta_ref, bdiag_ref, bprev_ref,
         lq1_ref, lk1_ref, lq2_ref, lk2_ref, subln_ref, o_ref, m_ref, acc_ref) = refs
    else:
        q_ref, k_ref, v_ref, kmeta_ref, vmeta_ref, bmeta_ref, bdiag_ref, o_ref, m_ref, acc_ref = refs
        bprev_ref = None
    qi = pl.program_id(2)
    nchunk = tq // rc
    nv = acc_ref.shape[1]
    chains = [(hh, mm, c) for hh in range(hp) for mm in range(nm) for c in range(nchunk)]
    qs = [q_ref[0, hh, mm, c * rc:(c + 1) * rc, :] for (hh, mm, c) in chains]

    def update(i, s, vt, first=False):
        rows = slice(i * rc, (i + 1) * rc)
        m_cur = jnp.max(s, axis=-1, keepdims=True)
        if first:
            m_new = jnp.broadcast_to(m_cur, (rc, LANES))
        else:
            m_prev = m_ref[rows]
            m_new = jnp.maximum(m_prev, m_cur)
        p = jnp.exp2((s - pltpu.repeat(m_new, s.shape[1] // LANES, axis=1)).astype(BF16))
        pv = _dot(p, vt)
        if first:
            acc_ref[rows] = pv
        else:
            alpha = jnp.exp2(m_prev - m_new)
            acc_ref[rows] = pltpu.repeat(alpha, nv // LANES, axis=1) * acc_ref[rows] + pv
        m_ref[rows] = m_new

    for i, (hh, mm, c) in enumerate(chains):
        s = _dot_nt(qs[i], kmeta_ref[hh]) + bmeta_ref[0, 0, c * rc:(c + 1) * rc, :]
        update(i, s, vmeta_ref[hh], first=True)

    def step(j, bias_ref=None):
        off = pl.multiple_of(j * tq, tq)
        scores = []
        for i, (hh, mm, c) in enumerate(chains):
            s = _dot_nt(qs[i], k_ref[0, hh, pl.ds(off, tq), :])
            if bias_ref is not None:
                s = s + bias_ref[0, c * rc:(c + 1) * rc, :]
            scores.append(s)
        for i, (hh, mm, c) in enumerate(chains):
            update(i, scores[i], v_ref[0, hh, pl.ds(off, tq), :])

    def plain(j, carry):
        step(j)
        return carry

    if diff:
        lax.fori_loop(0, jnp.maximum(qi - 1, 0), plain, 0)
        pl.when(qi > 0)(lambda: step(qi - 1, bprev_ref))
    else:
        lax.fori_loop(0, qi, plain, 0)
    step(qi, bdiag_ref)

    def normalized(i):
        acc = acc_ref[i * rc:(i + 1) * rc]
        return acc[:, :LANES] / acc[:, nv - 1:nv]

    outs = {ch: normalized(i) for i, ch in enumerate(chains)}
    if diff:
        lam = (jnp.exp(jnp.sum(lq1_ref[...] * lk1_ref[...], axis=-1, keepdims=True))
               - jnp.exp(jnp.sum(lq2_ref[...] * lk2_ref[...], axis=-1, keepdims=True)) + lambda_init)
        for c in range(nchunk):
            d = outs[(0, 0, c)] - lam * outs[(0, 1, c)]
            y = _rms(d, subln_ref[...]) * (1.0 - lambda_init)
            o_ref[0, c * rc:(c + 1) * rc, :] = y.astype(BF16)
    else:
        lo = lax.broadcasted_iota(jnp.int32, (1, LANES), 1) < MLA_V
        for c in range(nchunk):
            pair = jnp.where(lo, outs[(0, 0, c)], pltpu.roll(outs[(1, 0, c)], MLA_V, 1))
            o_ref[0, c * rc:(c + 1) * rc, :] = pair.astype(BF16)


def _flash(q, k, v, kmeta, vmeta, bmeta, bdiag, bprev, extra, *, diff, tq, lambda_init):
    b, g, nm, s, _ = q.shape
    nv = v.shape[-1]
    hp = 1 if diff else 2
    ngrid = g // hp
    nq = s // tq
    rows = tq
    rc = min(ATTN_CHAIN_ROWS, tq)
    grid = (b, ngrid, nq)
    gsel = (lambda gi: gi) if diff else (lambda gi: 0)
    in_specs = [
        pl.BlockSpec((1, hp, nm, tq, LANES), lambda bi, gi, qi: (bi, gi, 0, qi, 0)),
        pl.BlockSpec((1, hp, s, LANES), lambda bi, gi, qi: (bi, gi, 0, 0)),
        pl.BlockSpec((1, hp, s, nv), lambda bi, gi, qi: (bi, gi, 0, 0)),
        pl.BlockSpec((hp, LANES, LANES), lambda bi, gi, qi: (gi, 0, 0)),
        pl.BlockSpec((hp, LANES, nv), lambda bi, gi, qi: (gi, 0, 0)),
        pl.BlockSpec((1, 1, rows, LANES), lambda bi, gi, qi: (jnp.minimum(qi, 1), gsel(gi), 0, 0)),
        pl.BlockSpec((1, rows, tq), lambda bi, gi, qi: (gsel(gi), 0, 0)),
    ]
    args = [q, k, v, kmeta, vmeta, bmeta, bdiag]
    if diff:
        in_specs.append(pl.BlockSpec((1, rows, tq), lambda bi, gi, qi: (gi, 0, 0)))
        args.append(bprev)
        for e in extra:
            in_specs.append(pl.BlockSpec(e.shape, lambda bi, gi, qi: (0, 0)))
            args.append(e)
    kern = functools.partial(_flash_kernel, hp=hp, nm=nm, tq=tq, rc=rc, diff=diff, lambda_init=lambda_init)
    return pl.pallas_call(
        kern, grid=grid, in_specs=in_specs,
        out_specs=pl.BlockSpec((1, tq, LANES), lambda bi, gi, qi: (bi, qi, gi)),
        out_shape=jax.ShapeDtypeStruct((b, s, ngrid * LANES), BF16),
        scratch_shapes=[pltpu.VMEM((hp * nm * tq, LANES), F32), pltpu.VMEM((hp * nm * tq, nv), F32)],
        compiler_params=pltpu.CompilerParams(dimension_semantics=("arbitrary",) * 3,
                                             vmem_limit_bytes=VMEM_LIMIT),
        name="flash_diff" if diff else "flash_mla",
    )(*args)


def _top_values(s, n, row_ref=None):
    vals = []
    for i in range(n):
        m = jnp.max(s, axis=0, keepdims=True)
        vals.append(m)
        if row_ref is not None and i < PEER_TOPK:
            row_ref[i:i + 1, :] = m
        if i + 1 < n:
            s = jnp.where(s == m, -jnp.inf, s)
    return vals


def _front_kernel(x_ref, ym_ref, yd_ref, woa_ref, wob_ref, g2_ref, wqt_ref, sk_ref,
                  h1_ref, n2t_ref, a_ref, p_ref, nb_ref, r_ref, v2s_ref, cand_ref):
    h1 = x_ref[...] + _dot(ym_ref[...], woa_ref[...]) + _dot(yd_ref[...], wob_ref[...])
    h1_ref[...] = h1
    n2t = _rms(h1, g2_ref[...]).T.astype(BF16)
    n2t_ref[...] = n2t
    qpt = _dot(wqt_ref[...], n2t).astype(BF16)
    k = PEER_TOPK
    for h in range(PEER_HEADS):
        s1 = _dot(sk_ref[2 * h], qpt[(2 * h) * PEER_SUB:(2 * h + 1) * PEER_SUB])
        s2 = _dot(sk_ref[2 * h + 1], qpt[(2 * h + 1) * PEER_SUB:(2 * h + 2) * PEER_SUB])
        v1 = _top_values(s1, k + 1)
        v2 = _top_values(s2, k + 1, v2s_ref)
        v2s = v2s_ref[...]
        for i in range(k):
            cand_ref[i * k:(i + 1) * k, :] = v1[i] + v2s
        c = _top_values(cand_ref[...], k + 1)
        c17 = jnp.maximum(c[k], jnp.maximum(v1[k] + v2[0], v1[0] + v2[k]))
        tau = 0.5 * (c[k - 1] + c17)
        z = jnp.exp(c[0] - c[0])
        for i in range(1, k):
            z = z + jnp.exp(c[i] - c[0])
        a_ref[h] = s1 - tau
        nb_ref[h] = -s2
        p_ref[h] = jnp.exp(s1 - v1[0]) * (0.5 / z)
        r_ref[h] = jnp.exp(s2 - v2[0])


def _front(x2, ym, yd, woa, wob, g2, wqt, sk, rows):
    t, d = x2.shape
    rows = min(rows, t)
    grid = (t // rows,)
    full = lambda a: pl.BlockSpec(a.shape, lambda i: (0,) * a.ndim)
    gate = jax.ShapeDtypeStruct((PEER_HEADS, PEER_KEYS, t), F32)
    gate_spec = pl.BlockSpec((PEER_HEADS, PEER_KEYS, rows), lambda i: (0, 0, i))
    return pl.pallas_call(
        _front_kernel, grid=grid,
        in_specs=[pl.BlockSpec((rows, d), lambda i: (i, 0)),
                  pl.BlockSpec((rows, ym.shape[1]), lambda i: (i, 0)),
                  pl.BlockSpec((rows, yd.shape[1]), lambda i: (i, 0)),
                  full(woa), full(wob), full(g2), full(wqt), full(sk)],
        out_specs=(pl.BlockSpec((rows, d), lambda i: (i, 0)),
                   pl.BlockSpec((d, rows), lambda i: (0, i)),
                   gate_spec, gate_spec, gate_spec, gate_spec),
        out_shape=(jax.ShapeDtypeStruct((t, d), F32), jax.ShapeDtypeStruct((d, t), BF16),
                   gate, gate, gate, gate),
        scratch_shapes=[pltpu.VMEM((PEER_TOPK, rows), F32),
                        pltpu.VMEM((PEER_TOPK * PEER_TOPK, rows), F32)],
        compiler_params=pltpu.CompilerParams(dimension_semantics=("arbitrary",),
                                             vmem_limit_bytes=VMEM_LIMIT),
        name="front",
    )(x2, ym, yd, woa, wob, g2, wqt, sk)


def _dense_kernel(n2t_ref, u_ref, vt_ref, a_ref, p_ref, nb_ref, r_ref, h1_ref, o_ref, acc_ref, w_ref, pre_ref):
    e = pl.program_id(1)

    @pl.when(e == 0)
    def _():
        acc_ref[...] = jnp.zeros_like(acc_ref)

    n_tok = n2t_ref.shape[1]
    piece = DENSE_PIECE
    n_piece = u_ref.shape[0] // piece
    first_keys = piece // PEER_KEYS

    def pre_matmul(pc):
        rows = slice(pc * piece, (pc + 1) * piece)
        pre_ref[rows, :] = _dot(u_ref[rows, :], n2t_ref[...])

    def gate(pc):
        for tc in range(n_tok // LANES):
            cs = slice(tc * LANES, (tc + 1) * LANES)
            for jt in range(PEER_KEYS // GATE_ROWS):
                js = slice(jt * GATE_ROWS, (jt + 1) * GATE_ROWS)
                g = [None] * first_keys
                for h in range(PEER_HEADS):
                    nb = nb_ref[h, js, cs].reshape(GATE_ROWS // SUBLANES, SUBLANES, LANES)
                    r = r_ref[h, js, cs].reshape(GATE_ROWS // SUBLANES, SUBLANES, LANES)
                    for k in range(first_keys):
                        ii = pc * first_keys + k
                        a8 = jnp.broadcast_to(a_ref[h, ii:ii + 1, cs], (SUBLANES, LANES))
                        p8 = jnp.broadcast_to(p_ref[h, ii:ii + 1, cs], (SUBLANES, LANES))
                        term = jnp.where(a8[None] >= nb, p8[None] * r, 0.0)
                        g[k] = term if g[k] is None else g[k] + term
                for k in range(first_keys):
                    row0 = (pc * first_keys + k) * PEER_KEYS + jt * GATE_ROWS
                    pre = pre_ref[row0:row0 + GATE_ROWS, cs]
                    act = pre * (1.0 + lax.erf(pre * math.sqrt(0.5)))
                    w_ref[row0:row0 + GATE_ROWS, cs] = (g[k].reshape(GATE_ROWS, LANES) * act).astype(BF16)

    def out_matmul(pc):
        rows = slice(pc * piece, (pc + 1) * piece)
        acc_ref[...] += _dot(vt_ref[:, rows], w_ref[rows, :])

    pre_matmul(0)
    for pc in range(n_piece):
        if pc + 1 < n_piece:
            pre_matmul(pc + 1)
        gate(pc)
        out_matmul(pc)

    @pl.when(e == pl.num_programs(1) - 1)
    def _():
        o_ref[...] = h1_ref[...] + acc_ref[...].T


def _dense(n2t, u, vt, a, p, nb, r, h1, tokens, experts):
    d, t = n2t.shape
    ne = u.shape[0]
    tokens = min(tokens, t)
    grid = (t // tokens, ne // experts)
    n_i = experts // PEER_KEYS
    return pl.pallas_call(
        _dense_kernel, grid=grid,
        in_specs=[pl.BlockSpec((d, tokens), lambda ti, ei: (0, ti)),
                  pl.BlockSpec((experts, d), lambda ti, ei: (ei, 0)),
                  pl.BlockSpec((d, experts), lambda ti, ei: (0, ei)),
                  pl.BlockSpec((PEER_HEADS, n_i, tokens), lambda ti, ei: (0, ei, ti)),
                  pl.BlockSpec((PEER_HEADS, n_i, tokens), lambda ti, ei: (0, ei, ti)),
                  pl.BlockSpec((PEER_HEADS, PEER_KEYS, tokens), lambda ti, ei: (0, 0, ti)),
                  pl.BlockSpec((PEER_HEADS, PEER_KEYS, tokens), lambda ti, ei: (0, 0, ti)),
                  pl.BlockSpec((tokens, d), lambda ti, ei: (ti, 0))],
        out_specs=pl.BlockSpec((tokens, d), lambda ti, ei: (ti, 0)),
        out_shape=jax.ShapeDtypeStruct((t, d), F32),
        scratch_shapes=[pltpu.VMEM((d, tokens), F32), pltpu.VMEM((experts, tokens), BF16),
                        pltpu.VMEM((experts, tokens), F32)],
        compiler_params=pltpu.CompilerParams(dimension_semantics=("arbitrary", "arbitrary"),
                                             vmem_limit_bytes=VMEM_LIMIT),
        name="dense",
    )(n2t, u, vt, a, p, nb, r, h1)


def _rope_tables(pos):
    half = MLA_ROPE // 2
    inv_freq = ROPE_THETA ** (-jnp.arange(half, dtype=F32) / half)
    ang = pos.astype(F32)[:, None] * inv_freq[None, :]
    cos, sin = jnp.cos(ang), jnp.sin(ang)
    n = pos.shape[0]
    one = jnp.ones((n, MLA_NOPE), F32)
    zero = jnp.zeros((n, MLA_NOPE), F32)
    tail1 = jnp.ones((n, LANES - MLA_QK), F32)
    tail0 = jnp.zeros((n, LANES - MLA_QK), F32)
    zh = jnp.zeros((n, half), F32)
    cos_t = jnp.concatenate([one, cos, cos, tail1], axis=1)
    sina = jnp.concatenate([zero, -sin, zh, tail0], axis=1)
    sinb = jnp.concatenate([zero, zh, sin, tail0], axis=1)
    return cos_t, sina, sinb


def _bucket_np(dist):
    n = np.maximum(dist, 0)
    max_exact = NUM_BUCKETS // 2
    nf = np.maximum(n, 1).astype(np.float32)
    large = max_exact + (np.log(nf / np.float32(max_exact)) / np.float32(math.log(MAX_DISTANCE / max_exact))
                         * np.float32(NUM_BUCKETS - max_exact)).astype(np.int32)
    large = np.minimum(large, NUM_BUCKETS - 1)
    return np.where(n < max_exact, n, large)


def _bias_tiles(rel_bias, tq):
    far = int(np.min(np.nonzero(_bucket_np(np.arange(4 * MAX_DISTANCE)) == NUM_BUCKETS - 1)[0]))
    assert np.all(_bucket_np(np.arange(far, 8 * MAX_DISTANCE)) == NUM_BUCKETS - 1)
    if tq < far:
        raise NotImplementedError("attention tile smaller than the relative-bias window")
    table = rel_bias.astype(F32)
    shifted = (table - table[NUM_BUCKETS - 1][None, :]) * LOG2E

    def toeplitz(n_rows, n_cols, off):
        period = n_rows + n_cols - 1
        dist = np.arange(period) - (n_cols - 1) + off
        vals = jnp.where((dist >= 0)[:, None], shifted[_bucket_np(np.maximum(dist, 0))], NEG)
        u = vals[::-1].T
        flat = jnp.tile(u, (1, n_rows + 1))[:, :n_rows * (period + 1)]
        return flat.reshape(-1, n_rows, period + 1)[:, ::-1, :n_cols]

    diag = toeplitz(tq, tq, 0)
    prev = toeplitz(tq, tq, tq)
    pad = jnp.asarray(_mask_tiles(tq)[0][:, 0])
    nh = rel_bias.shape[1]
    meta0 = jnp.where(pad[0][None] < 0.0, NEG, toeplitz(tq, LANES, N_META))
    meta_far = jnp.broadcast_to(pad[1][None], (nh, tq, LANES))
    return jnp.stack([meta0, meta_far]), diag, prev


def _mask_tiles(tq):
    r = np.arange(tq)[:, None]
    c = np.arange(tq)[None, :]
    diag = np.where(r >= c, 0.0, NEG).astype(np.float32)[None]
    cm = np.arange(LANES)[None, :]
    meta = np.broadcast_to(np.where(cm < N_META, 0.0, NEG).astype(np.float32), (tq, LANES))
    return np.stack([meta, meta])[:, None], diag


def _pad_lanes(a, width):
    return jnp.pad(a, [(0, 0)] * (a.ndim - 1) + [(0, width - a.shape[-1])])


def kernel(x, meta_tokens, rel_bias, attn_norm, w_in, mla_q_norm, mla_w_uq, mla_kv_norm, mla_w_ukv,
           mla_qk_norm_q, mla_qk_norm_k, diff_q_norm, diff_k_norm, diff_lambda_q1, diff_lambda_k1,
           diff_lambda_q2, diff_lambda_k2, diff_subln, w_out, ffn_norm, peer_w_query, peer_sub_keys,
           peer_u, peer_v):
    b, s, d = x.shape
    assert attn_norm.shape[0] == 1, "meta rows are only used as keys: single layer"
    lambda_init = 0.8 - 0.6 * math.exp(-0.3 * 0)
    tq = min(ATTN_TILE, s)
    assert s % tq == 0

    wi = w_in[0]
    s1, s2, s3 = MLA_Q_RANK, MLA_Q_RANK + MLA_KV_RANK, MLA_Q_RANK + MLA_KV_RANK + MLA_ROPE
    wkr = jnp.pad(wi[:, s2:s3], ((0, 0), (MLA_NOPE, LANES - MLA_QK)))
    w_in_arr = jnp.concatenate([wi[:, :s2], wkr, wi[:, s3:]], axis=1).astype(BF16)
    wuq = _pad_lanes(mla_w_uq[0].reshape(MLA_Q_RANK, MLA_HEADS, MLA_QK), LANES).reshape(MLA_Q_RANK, -1).astype(BF16)
    wukv = mla_w_ukv[0].reshape(MLA_KV_RANK, MLA_HEADS, MLA_NOPE + MLA_V)
    wk = _pad_lanes(wukv[..., :MLA_NOPE], LANES).reshape(MLA_KV_RANK, -1)
    wv = _pad_lanes(wukv[..., MLA_NOPE:], LANES).reshape(MLA_KV_RANK, -1)
    wkv = jnp.concatenate([wk, wv], axis=1).astype(BF16)
    row = lambda a: a.reshape(1, -1).astype(F32)
    gq = _pad_lanes(row(mla_qk_norm_q[0]), LANES)
    gk = _pad_lanes(row(mla_qk_norm_k[0]), LANES)
    gdq = jnp.tile(row(diff_q_norm[0]), (1, 2))
    gdk = jnp.tile(row(diff_k_norm[0]), (1, 2))
    wts = (row(attn_norm[0]), w_in_arr, row(mla_q_norm[0]), wuq, row(mla_kv_norm[0]), wkv, gq, gk, gdq, gdk)

    pos_real = jnp.arange(N_META, N_META + s, dtype=jnp.int32)
    qm, km, vm, qd, kd, vd = _project(x, _rope_tables(pos_real), wts, PROJ_ROWS)
    pos_meta = jnp.arange(N_META, dtype=jnp.int32)
    _, km_m, vm_m, _, kd_m, vd_m = _project(meta_tokens[None].astype(x.dtype), _rope_tables(pos_meta), wts, N_META)
    padk = lambda a: jnp.pad(a[0], ((0, 0), (0, LANES - N_META), (0, 0)))

    mmeta, mdiag = (jnp.asarray(a) for a in _mask_tiles(tq))
    y_mla = _flash(qm[:, :, None], km, vm, padk(km_m), padk(vm_m), mmeta, mdiag, None, (),
                   diff=False, tq=tq, lambda_init=lambda_init)
    bmeta, bdiag, bprev = _bias_tiles(rel_bias, tq)
    extra = (row(diff_lambda_q1[0]), row(diff_lambda_k1[0]), row(diff_lambda_q2[0]), row(diff_lambda_k2[0]),
             row(diff_subln[0]))
    y_diff = _flash(qd, kd, vd, padk(kd_m), padk(vd_m), bmeta, bdiag, bprev, extra,
                    diff=True, tq=tq, lambda_init=lambda_init)

    t = b * s
    wo = w_out[0].astype(BF16)
    n_mla = MLA_HEADS * MLA_V
    wqt = peer_w_query[0].T.astype(BF16)
    sk = peer_sub_keys[0].reshape(PEER_HEADS * 2, PEER_KEYS, PEER_SUB).astype(BF16)
    h1, n2t, ga, gp, gnb, gr = _front(x.reshape(t, d), y_mla.reshape(t, -1), y_diff.reshape(t, -1),
                                      wo[:n_mla], wo[n_mla:], row(ffn_norm[0]), wqt, sk, FRONT_ROWS)
    u_bf = peer_u[0].astype(BF16)
    vt_bf = peer_v[0].T.astype(BF16)
    out = _dense(n2t, u_bf, vt_bf, ga, gp, gnb, gr, h1, DENSE_TOKENS, DENSE_EXPERTS)
    return out.reshape(b, s, d)
```

```python
import functools
import math

import numpy as np
import jax
import jax.numpy as jnp
from jax import lax
from jax.experimental import pallas as pl
from jax.experimental.pallas import tpu as pltpu

F32 = jnp.float32
BF16 = jnp.bfloat16

N_META = 16
EPS = 1e-6
LANES = 128
SUBLANES = 8
PACKED_ROWS = 16

MLA_HEADS = 8
MLA_Q_RANK = 256
MLA_KV_RANK = 256
MLA_NOPE = 64
MLA_ROPE = 32
MLA_V = 64
MLA_QK = MLA_NOPE + MLA_ROPE
ROPE_THETA = 10000.0

DIFF_HEADS = 4
DIFF_HD = 64
DIFF_V = 2 * DIFF_HD

NUM_BUCKETS = 32
MAX_DISTANCE = 128

PEER_HEADS = 8
PEER_KEYS = 128
PEER_TOPK = 16
PEER_SUB = 128

NEG = -1e30

PROJ_ROWS = 256
ATTN_TILE = 512
ATTN_CHAIN_ROWS = 256
LOG2E = math.log2(math.e)
FRONT_ROWS = 256
DENSE_TOKENS = 512
DENSE_EXPERTS = 1024
DENSE_PIECE = 256
GATE_ROWS = 64
VMEM_LIMIT = 56 * 1024 * 1024


def _rms(x, g):
    return x * lax.rsqrt(jnp.mean(x * x, axis=-1, keepdims=True) + EPS) * g


def _dot(a, b):
    return jnp.dot(a, b, preferred_element_type=F32)


def _dot_nt(a, b):
    return lax.dot_general(a, b, (((1,), (1,)), ((), ())), preferred_element_type=F32)


def _proj_kernel(x_ref, g1_ref, win_ref, gcq_ref, wuq_ref, gckv_ref, wkv_ref, gq_ref, gk_ref,
                 gdq_ref, gdk_ref, cos_ref, sina_ref, sinb_ref,
                 qm_ref, km_ref, vm_ref, qd_ref, kd_ref, vd_ref, *, scale_m, scale_d):
    x = x_ref[0]
    n = _rms(x, g1_ref[...])
    proj = _dot(n.astype(BF16), win_ref[...])
    cq = proj[:, 0:256]
    ckv = proj[:, 256:512]
    krp = proj[:, 512:640]
    dq = proj[:, 640:1152]
    dk = proj[:, 1152:1664]
    dv = proj[:, 1664:2176]
    q_raw = _dot(_rms(cq, gcq_ref[...]).astype(BF16), wuq_ref[...])
    kv = _dot(_rms(ckv, gckv_ref[...]).astype(BF16), wkv_ref[...])
    cos = cos_ref[...]
    sina = sina_ref[...]
    sinb = sinb_ref[...]
    gq = gq_ref[...]
    gk = gk_ref[...]

    def rope(t):
        return t * cos + pltpu.roll(t, LANES - 16, 1) * sina + pltpu.roll(t, 16, 1) * sinb

    inv_qk = 1.0 / MLA_QK
    for h in range(MLA_HEADS):
        sl = slice(h * LANES, (h + 1) * LANES)
        qh = q_raw[:, sl]
        rq = lax.rsqrt(jnp.sum(qh * qh, axis=-1, keepdims=True) * inv_qk + EPS)
        qm_ref[0, h] = (rope(qh * rq * gq) * scale_m).astype(BF16)
        kh = kv[:, sl] + krp
        rk = lax.rsqrt(jnp.sum(kh * kh, axis=-1, keepdims=True) * inv_qk + EPS)
        km_ref[0, h] = rope(kh * rk * gk).astype(BF16)
    lo_v = lax.broadcasted_iota(jnp.int32, (1, LANES), 1) < MLA_V
    for h in range(MLA_HEADS):
        vm_ref[0, h] = jnp.where(lo_v, kv[:, 1024 + h * LANES:1024 + (h + 1) * LANES], 1.0).astype(BF16)

    lo = lax.broadcasted_iota(jnp.int32, (1, LANES), 1) < DIFF_HD
    inv_hd = 1.0 / DIFF_HD

    def halfnorm(t, g):
        sq = t * t
        s_lo = jnp.sum(jnp.where(lo, sq, 0.0), axis=-1, keepdims=True)
        s_hi = jnp.sum(jnp.where(lo, 0.0, sq), axis=-1, keepdims=True)
        r = jnp.where(lo, lax.rsqrt(s_lo * inv_hd + EPS), lax.rsqrt(s_hi * inv_hd + EPS))
        return t * r * g

    gdq = gdq_ref[...]
    gdk = gdk_ref[...]
    for h in range(DIFF_HEADS):
        sl = slice(h * LANES, (h + 1) * LANES)
        qn = halfnorm(dq[:, sl], gdq) * scale_d
        qd_ref[0, h, 0] = jnp.where(lo, qn, 0.0).astype(BF16)
        qd_ref[0, h, 1] = jnp.where(lo, 0.0, qn).astype(BF16)
        kd_ref[0, h] = halfnorm(dk[:, sl], gdk).astype(BF16)
        vd_ref[0, h, :, 0:LANES] = dv[:, sl].astype(BF16)
        vd_ref[0, h, :, LANES:2 * LANES] = jnp.ones_like(dv[:, sl]).astype(BF16)


def _project(x3, tabs, wts, rows):
    b, s, d = x3.shape
    rows = min(rows, s)
    assert s % rows == 0
    grid = (b, s // rows)
    full = lambda a: pl.BlockSpec(a.shape, lambda i, j: (0,) * a.ndim)
    tab_spec = pl.BlockSpec((rows, LANES), lambda i, j: (j, 0))
    in_specs = [pl.BlockSpec((1, rows, d), lambda i, j: (i, j, 0))] + [full(w) for w in wts] + [tab_spec] * 3
    hs = lambda nh: pl.BlockSpec((1, nh, rows, LANES), lambda i, j: (i, 0, j, 0))
    out_shape = (
        jax.ShapeDtypeStruct((b, MLA_HEADS, s, LANES), BF16),
        jax.ShapeDtypeStruct((b, MLA_HEADS, s, LANES), BF16),
        jax.ShapeDtypeStruct((b, MLA_HEADS, s, LANES), BF16),
        jax.ShapeDtypeStruct((b, DIFF_HEADS, 2, s, LANES), BF16),
        jax.ShapeDtypeStruct((b, DIFF_HEADS, s, LANES), BF16),
        jax.ShapeDtypeStruct((b, DIFF_HEADS, s, 2 * LANES), BF16),
    )
    out_specs = (hs(MLA_HEADS), hs(MLA_HEADS), hs(MLA_HEADS),
                 pl.BlockSpec((1, DIFF_HEADS, 2, rows, LANES), lambda i, j: (i, 0, 0, j, 0)),
                 hs(DIFF_HEADS),
                 pl.BlockSpec((1, DIFF_HEADS, rows, 2 * LANES), lambda i, j: (i, 0, j, 0)))
    kern = functools.partial(_proj_kernel, scale_m=MLA_QK ** -0.5 * LOG2E, scale_d=DIFF_HD ** -0.5 * LOG2E)
    return pl.pallas_call(
        kern, grid=grid, in_specs=in_specs, out_specs=out_specs, out_shape=out_shape,
        compiler_params=pltpu.CompilerParams(dimension_semantics=("arbitrary", "arbitrary"),
                                             vmem_limit_bytes=VMEM_LIMIT),
        name="proj",
    )(x3, *wts, *tabs)


def _flash_kernel(*refs, hp, nm, tq, rc, diff, lambda_init):
    if diff:
        (q_ref, k_ref, v_ref, kmeta_ref, vmeta_ref, bmeta_ref, bdiag_ref, bprev_ref,
         lq1_ref, lk1_ref, lq2_ref, lk2_ref, subln_ref, o_ref, m_ref, acc_ref) = refs
    else:
        q_ref, k_ref, v_ref, kmeta_ref, vmeta_ref, bmeta_ref, bdiag_ref, o_ref, m_ref, acc_ref = refs
        bprev_ref = None
    qi = pl.program_id(2)
    nchunk = tq // rc
    nv = acc_ref.shape[1]
    chains = [(hh, mm, c) for hh in range(hp) for mm in range(nm) for c in range(nchunk)]
    qs = [q_ref[0, hh, mm, c * rc:(c + 1) * rc, :] for (hh, mm, c) in chains]

    def update(i, s, vt, first=False):
        rows = slice(i * rc, (i + 1) * rc)
        m_cur = jnp.max(s, axis=-1, keepdims=True)
        if first:
            m_new = jnp.broadcast_to(m_cur, (rc, LANES))
        else:
            m_prev = m_ref[rows]
            m_new = jnp.maximum(m_prev, m_cur)
        p = jnp.exp2((s - pltpu.repeat(m_new, s.shape[1] // LANES, axis=1)).astype(BF16))
        pv = _dot(p, vt)
        if first:
            acc_ref[rows] = pv
        else:
            alpha = jnp.exp2(m_prev - m_new)
            acc_ref[rows] = pltpu.repeat(alpha, nv // LANES, axis=1) * acc_ref[rows] + pv
        m_ref[rows] = m_new

    for i, (hh, mm, c) in enumerate(chains):
        s = _dot_nt(qs[i], kmeta_ref[hh]) + bmeta_ref[0, 0, c * rc:(c + 1) * rc, :]
        update(i, s, vmeta_ref[hh], first=True)

    def step(j, bias_ref=None):
        off = pl.multiple_of(j * tq, tq)
        scores = []
        for i, (hh, mm, c) in enumerate(chains):
            s = _dot_nt(qs[i], k_ref[0, hh, pl.ds(off, tq), :])
            if bias_ref is not None:
                s = s + bias_ref[0, c * rc:(c + 1) * rc, :]
            scores.append(s)
        for i, (hh, mm, c) in enumerate(chains):
            update(i, scores[i], v_ref[0, hh, pl.ds(off, tq), :])

    def plain(j, carry):
        step(j)
        return carry

    if diff:
        lax.fori_loop(0, jnp.maximum(qi - 1, 0), plain, 0)
        pl.when(qi > 0)(lambda: step(qi - 1, bprev_ref))
    else:
        lax.fori_loop(0, qi, plain, 0)
    step(qi, bdiag_ref)

    def normalized(i):
        acc = acc_ref[i * rc:(i + 1) * rc]
        return acc[:, :LANES] / acc[:, nv - 1:nv]

    outs = {ch: normalized(i) for i, ch in enumerate(chains)}
    if diff:
        lam = (jnp.exp(jnp.sum(lq1_ref[...] * lk1_ref[...], axis=-1, keepdims=True))
               - jnp.exp(jnp.sum(lq2_ref[...] * lk2_ref[...], axis=-1, keepdims=True)) + lambda_init)
        for c in range(nchunk):
            d = outs[(0, 0, c)] - lam * outs[(0, 1, c)]
            y = _rms(d, subln_ref[...]) * (1.0 - lambda_init)
            o_ref[0, c * rc:(c + 1) * rc, :] = y.astype(BF16)
    else:
        lo = lax.broadcasted_iota(jnp.int32, (1, LANES), 1) < MLA_V
        for c in range(nchunk):
            pair = jnp.where(lo, outs[(0, 0, c)], pltpu.roll(outs[(1, 0, c)], MLA_V, 1))
            o_ref[0, c * rc:(c + 1) * rc, :] = pair.astype(BF16)


def _flash(q, k, v, kmeta, vmeta, bmeta, bdiag, bprev, extra, *, diff, tq, lambda_init):
    b, g, nm, s, _ = q.shape
    nv = v.shape[-1]
    hp = 1 if diff else 2
    ngrid = g // hp
    nq = s // tq
    rows = tq
    rc = min(ATTN_CHAIN_ROWS, tq)
    grid = (b, ngrid, nq)
    gsel = (lambda gi: gi) if diff else (lambda gi: 0)
    in_specs = [
        pl.BlockSpec((1, hp, nm, tq, LANES), lambda bi, gi, qi: (bi, gi, 0, qi, 0)),
        pl.BlockSpec((1, hp, s, LANES), lambda bi, gi, qi: (bi, gi, 0, 0)),
        pl.BlockSpec((1, hp, s, nv), lambda bi, gi, qi: (bi, gi, 0, 0)),
        pl.BlockSpec((hp, LANES, LANES), lambda bi, gi, qi: (gi, 0, 0)),
        pl.BlockSpec((hp, LANES, nv), lambda bi, gi, qi: (gi, 0, 0)),
        pl.BlockSpec((1, 1, rows, LANES), lambda bi, gi, qi: (jnp.minimum(qi, 1), gsel(gi), 0, 0)),
        pl.BlockSpec((1, rows, tq), lambda bi, gi, qi: (gsel(gi), 0, 0)),
    ]
    args = [q, k, v, kmeta, vmeta, bmeta, bdiag]
    if diff:
        in_specs.append(pl.BlockSpec((1, rows, tq), lambda bi, gi, qi: (gi, 0, 0)))
        args.append(bprev)
        for e in extra:
            in_specs.append(pl.BlockSpec(e.shape, lambda bi, gi, qi: (0, 0)))
            args.append(e)
    kern = functools.partial(_flash_kernel, hp=hp, nm=nm, tq=tq, rc=rc, diff=diff, lambda_init=lambda_init)
    return pl.pallas_call(
        kern, grid=grid, in_specs=in_specs,
        out_specs=pl.BlockSpec((1, tq, LANES), lambda bi, gi, qi: (bi, qi, gi)),
        out_shape=jax.ShapeDtypeStruct((b, s, ngrid * LANES), BF16),
        scratch_shapes=[pltpu.VMEM((hp * nm * tq, LANES), F32), pltpu.VMEM((hp * nm * tq, nv), F32)],
        compiler_params=pltpu.CompilerParams(dimension_semantics=("arbitrary",) * 3,
                                             vmem_limit_bytes=VMEM_LIMIT),
        name="flash_diff" if diff else "flash_mla",
    )(*args)


def _top_values(s, n, row_ref=None, want_rank=False):
    vals = []
    rank = jnp.full(s.shape, float(PEER_TOPK), F32) if want_rank else None
    for i in range(n):
        m = jnp.max(s, axis=0, keepdims=True)
        vals.append(m)
        if row_ref is not None and i < PEER_TOPK:
            row_ref[i:i + 1, :] = m
        if i + 1 < n:
            hit = s == m
            if want_rank and i < PEER_TOPK:
                rank = jnp.where(hit, float(i), rank)
            s = jnp.where(hit, -jnp.inf, s)
    return (vals, rank) if want_rank else vals


def _front_kernel(x_ref, ym_ref, yd_ref, woa_ref, wob_ref, g2_ref, wqt_ref, sk_ref,
                  h1_ref, n2t_ref, cnt_ref, p_ref, rank_ref, r_ref, v2s_ref, cand_ref):
    h1 = x_ref[...] + _dot(ym_ref[...], woa_ref[...]) + _dot(yd_ref[...], wob_ref[...])
    h1_ref[...] = h1
    n2t = _rms(h1, g2_ref[...]).T.astype(BF16)
    n2t_ref[...] = n2t
    qpt = _dot(wqt_ref[...], n2t).astype(BF16)
    k = PEER_TOPK
    for h in range(PEER_HEADS):
        s1 = _dot(sk_ref[2 * h], qpt[(2 * h) * PEER_SUB:(2 * h + 1) * PEER_SUB])
        s2 = _dot(sk_ref[2 * h + 1], qpt[(2 * h + 1) * PEER_SUB:(2 * h + 2) * PEER_SUB])
        v1 = _top_values(s1, k + 1)
        v2, rank2 = _top_values(s2, k + 1, v2s_ref, want_rank=True)
        v2s = v2s_ref[...]
        for i in range(k):
            cand_ref[i * k:(i + 1) * k, :] = v1[i] + v2s
        c = _top_values(cand_ref[...], k + 1)
        c17 = jnp.maximum(c[k], jnp.maximum(v1[k] + v2[0], v1[0] + v2[k]))
        tau = 0.5 * (c[k - 1] + c17)
        z = jnp.exp(c[0] - c[0])
        for i in range(1, k):
            z = z + jnp.exp(c[i] - c[0])
        cnt = jnp.zeros_like(s1)
        for l in range(k):
            cnt = cnt + jnp.where(s1 + v2[l] >= tau, 1.0, 0.0)
        cnt_ref[h] = cnt
        rank_ref[h] = rank2.astype(BF16)
        p_ref[h] = jnp.exp(s1 - v1[0]) * (0.5 / z)
        r_ref[h] = jnp.exp(s2 - v2[0]).astype(BF16)


def _front(x2, ym, yd, woa, wob, g2, wqt, sk, rows):
    t, d = x2.shape
    rows = min(rows, t)
    grid = (t // rows,)
    full = lambda a: pl.BlockSpec(a.shape, lambda i: (0,) * a.ndim)
    gate = jax.ShapeDtypeStruct((PEER_HEADS, PEER_KEYS, t), F32)
    gate16 = jax.ShapeDtypeStruct((PEER_HEADS, PEER_KEYS, t), BF16)
    gate_spec = pl.BlockSpec((PEER_HEADS, PEER_KEYS, rows), lambda i: (0, 0, i))
    return pl.pallas_call(
        _front_kernel, grid=grid,
        in_specs=[pl.BlockSpec((rows, d), lambda i: (i, 0)),
                  pl.BlockSpec((rows, ym.shape[1]), lambda i: (i, 0)),
                  pl.BlockSpec((rows, yd.shape[1]), lambda i: (i, 0)),
                  full(woa), full(wob), full(g2), full(wqt), full(sk)],
        out_specs=(pl.BlockSpec((rows, d), lambda i: (i, 0)),
                   pl.BlockSpec((d, rows), lambda i: (0, i)),
                   gate_spec, gate_spec, gate_spec, gate_spec),
        out_shape=(jax.ShapeDtypeStruct((t, d), F32), jax.ShapeDtypeStruct((d, t), BF16),
                   gate, gate, gate16, gate16),
        scratch_shapes=[pltpu.VMEM((PEER_TOPK, rows), F32),
                        pltpu.VMEM((PEER_TOPK * PEER_TOPK, rows), F32)],
        compiler_params=pltpu.CompilerParams(dimension_semantics=("arbitrary",),
                                             vmem_limit_bytes=VMEM_LIMIT),
        name="front",
    )(x2, ym, yd, woa, wob, g2, wqt, sk)


def _dense_kernel(n2t_ref, u0_ref, un_ref, vtp_ref, vtl_ref, cnt_ref, p_ref, rank_ref, r_ref, h1_ref, o_ref,
                  acc_ref, w0_ref, w1_ref, pre0_ref, pre1_ref, bcnt_ref, bp_ref, *, n_chunk):
    w_refs = (w0_ref, w1_ref)
    pre_refs = (pre0_ref, pre1_ref)
    e = pl.program_id(1)
    last = n_chunk - 1
    n_tok = n2t_ref.shape[1]
    n_exp = un_ref.shape[0]
    piece = DENSE_PIECE
    n_piece = n_exp // piece
    first_keys = piece // PEER_KEYS
    n_i = n_exp // PEER_KEYS
    d_piece = acc_ref.shape[0] // n_piece

    @pl.when(e == 0)
    def _():
        acc_ref[...] = jnp.zeros_like(acc_ref)
        pre0_ref[...] = _dot(u0_ref[...], n2t_ref[...])
        w1_ref[...] = jnp.zeros_like(w1_ref)

    half_tok = n_tok // 2

    def pre_next(pc, hf, nxt):
        rows = slice(pc * piece, (pc + 1) * piece)
        cols = slice(hf * half_tok, (hf + 1) * half_tok)
        pre_refs[nxt][rows, cols] = _dot(un_ref[rows, :], n2t_ref[:, cols])

    for h in range(PEER_HEADS):
        for ii in range(n_i):
            bcnt_ref[h * n_i + ii] = jnp.broadcast_to(cnt_ref[h, ii:ii + 1, :], (PACKED_ROWS, n_tok)).astype(BF16)
            bp_ref[h * n_i + ii] = jnp.broadcast_to(p_ref[h, ii:ii + 1, :], (PACKED_ROWS, n_tok)).astype(BF16)

    def gate(pc, tcs, cur):
        i0 = pc * first_keys
        nvr = GATE_ROWS // PACKED_ROWS
        for tc in tcs:
            cs = slice(tc * LANES, (tc + 1) * LANES)
            for jt in range(PEER_KEYS // GATE_ROWS):
                js = slice(jt * GATE_ROWS, (jt + 1) * GATE_ROWS)
                g = [None] * first_keys
                for h in range(PEER_HEADS):
                    rank = rank_ref[h, js, cs]
                    r = r_ref[h, js, cs]
                    for k in range(first_keys):
                        cnt = jnp.concatenate([bcnt_ref[h * n_i + i0 + k, :, cs]] * nvr, axis=0)
                        p = jnp.concatenate([bp_ref[h * n_i + i0 + k, :, cs]] * nvr, axis=0)
                        sel = jnp.minimum(jnp.maximum(cnt - rank, 0.0), 1.0)
                        term = sel * (p * r)
                        g[k] = term if g[k] is None else g[k] + term
                for k in range(first_keys):
                    row0 = (i0 + k) * PEER_KEYS + jt * GATE_ROWS
                    pre = pre_refs[cur][row0:row0 + GATE_ROWS, cs]
                    act = pre * (1.0 + lax.erf(pre * math.sqrt(0.5)))
                    w_refs[cur][row0:row0 + GATE_ROWS, cs] =g[k] * act.astype(BF16)

    def out_prev(pc, hf, nxt):
        rows = slice(pc * d_piece, (pc + 1) * d_piece)
        cols = slice(hf * half_tok, (hf + 1) * half_tok)
        acc_ref[rows, cols] += _dot(vtp_ref[rows, :], w_refs[nxt][:, cols])

    def body(cur):
        n_tc = n_tok // LANES
        mxu_items = [functools.partial(pre_next, hf=0), functools.partial(pre_next, hf=1),
                     functools.partial(out_prev, hf=0), functools.partial(out_prev, hf=1)]
        for pc in range(n_piece):
            for s, item in enumerate(mxu_items):
                item(pc, nxt=1 - cur)
                gate(pc, range(s * n_tc // 4, (s + 1) * n_tc // 4), cur)

    for parity in (0, 1):
        pl.when(e % 2 == parity)(functools.partial(body, parity))

    @pl.when(e == last)
    def _():
        acc = acc_ref[...] + _dot(vtl_ref[...], w_refs[last % 2][...])
        o_ref[...] = h1_ref[...] + acc.T


def _dense(n2t, u, vt, cnt, p, rank, r, h1, tokens, experts):
    d, t = n2t.shape
    ne = u.shape[0]
    tokens = min(tokens, t)
    n_chunk = ne // experts
    grid = (t // tokens, n_chunk)
    n_i = experts // PEER_KEYS
    return pl.pallas_call(
        functools.partial(_dense_kernel, n_chunk=n_chunk), grid=grid,
        in_specs=[pl.BlockSpec((d, tokens), lambda ti, ei: (0, ti)),
                  pl.BlockSpec((experts, d), lambda ti, ei: (0, 0)),
                  pl.BlockSpec((experts, d), lambda ti, ei: (jnp.minimum(ei + 1, n_chunk - 1), 0)),
                  pl.BlockSpec((d, experts), lambda ti, ei: (0, jnp.maximum(ei - 1, 0))),
                  pl.BlockSpec((d, experts), lambda ti, ei: (0, jnp.where(ei == n_chunk - 1, n_chunk - 1, 0))),
                  pl.BlockSpec((PEER_HEADS, n_i, tokens), lambda ti, ei: (0, ei, ti)),
                  pl.BlockSpec((PEER_HEADS, n_i, tokens), lambda ti, ei: (0, ei, ti)),
                  pl.BlockSpec((PEER_HEADS, PEER_KEYS, tokens), lambda ti, ei: (0, 0, ti)),
                  pl.BlockSpec((PEER_HEADS, PEER_KEYS, tokens), lambda ti, ei: (0, 0, ti)),
                  pl.BlockSpec((tokens, d), lambda ti, ei: (ti, 0))],
        out_specs=pl.BlockSpec((tokens, d), lambda ti, ei: (ti, 0)),
        out_shape=jax.ShapeDtypeStruct((t, d), F32),
        scratch_shapes=[pltpu.VMEM((d, tokens), F32),
                        pltpu.VMEM((experts, tokens), BF16), pltpu.VMEM((experts, tokens), BF16),
                        pltpu.VMEM((experts, tokens), F32), pltpu.VMEM((experts, tokens), F32),
                        pltpu.VMEM((PEER_HEADS * n_i, PACKED_ROWS, tokens), BF16),
                        pltpu.VMEM((PEER_HEADS * n_i, PACKED_ROWS, tokens), BF16)],
        compiler_params=pltpu.CompilerParams(dimension_semantics=("arbitrary", "arbitrary"),
                                             vmem_limit_bytes=VMEM_LIMIT),
        name="dense",
    )(n2t, u, u, vt, vt, cnt, p, rank, r, h1)


def _rope_tables(pos):
    half = MLA_ROPE // 2
    inv_freq = ROPE_THETA ** (-jnp.arange(half, dtype=F32) / half)
    ang = pos.astype(F32)[:, None] * inv_freq[None, :]
    cos, sin = jnp.cos(ang), jnp.sin(ang)
    n = pos.shape[0]
    one = jnp.ones((n, MLA_NOPE), F32)
    zero = jnp.zeros((n, MLA_NOPE), F32)
    tail1 = jnp.ones((n, LANES - MLA_QK), F32)
    tail0 = jnp.zeros((n, LANES - MLA_QK), F32)
    zh = jnp.zeros((n, half), F32)
    cos_t = jnp.concatenate([one, cos, cos, tail1], axis=1)
    sina = jnp.concatenate([zero, -sin, zh, tail0], axis=1)
    sinb = jnp.concatenate([zero, zh, sin, tail0], axis=1)
    return cos_t, sina, sinb


def _bucket_np(dist):
    n = np.maximum(dist, 0)
    max_exact = NUM_BUCKETS // 2
    nf = np.maximum(n, 1).astype(np.float32)
    large = max_exact + (np.log(nf / np.float32(max_exact)) / np.float32(math.log(MAX_DISTANCE / max_exact))
                         * np.float32(NUM_BUCKETS - max_exact)).astype(np.int32)
    large = np.minimum(large, NUM_BUCKETS - 1)
    return np.where(n < max_exact, n, large)


def _bias_tiles(rel_bias, tq):
    far = int(np.min(np.nonzero(_bucket_np(np.arange(4 * MAX_DISTANCE)) == NUM_BUCKETS - 1)[0]))
    assert np.all(_bucket_np(np.arange(far, 8 * MAX_DISTANCE)) == NUM_BUCKETS - 1)
    if tq < far:
        raise NotImplementedError("attention tile smaller than the relative-bias window")
    table = rel_bias.astype(F32)
    shifted = (table - table[NUM_BUCKETS - 1][None, :]) * LOG2E

    def toeplitz(n_rows, n_cols, off):
        period = n_rows + n_cols - 1
        dist = np.arange(period) - (n_cols - 1) + off
        vals = jnp.where((dist >= 0)[:, None], shifted[_bucket_np(np.maximum(dist, 0))], NEG)
        u = vals[::-1].T
        flat = jnp.tile(u, (1, n_rows + 1))[:, :n_rows * (period + 1)]
        return flat.reshape(-1, n_rows, period + 1)[:, ::-1, :n_cols]

    diag = toeplitz(tq, tq, 0)
    prev = toeplitz(tq, tq, tq)
    pad = jnp.asarray(_mask_tiles(tq)[0][:, 0])
    nh = rel_bias.shape[1]
    meta0 = jnp.where(pad[0][None] < 0.0, NEG, toeplitz(tq, LANES, N_META))
    meta_far = jnp.broadcast_to(pad[1][None], (nh, tq, LANES))
    return jnp.stack([meta0, meta_far]), diag, prev


def _mask_tiles(tq):
    r = np.arange(tq)[:, None]
    c = np.arange(tq)[None, :]
    diag = np.where(r >= c, 0.0, NEG).astype(np.float32)[None]
    cm = np.arange(LANES)[None, :]
    meta = np.broadcast_to(np.where(cm < N_META, 0.0, NEG).astype(np.float32), (tq, LANES))
    return np.stack([meta, meta])[:, None], diag


def _pad_lanes(a, width):
    return jnp.pad(a, [(0, 0)] * (a.ndim - 1) + [(0, width - a.shape[-1])])


def kernel(x, meta_tokens, rel_bias, attn_norm, w_in, mla_q_norm, mla_w_uq, mla_kv_norm, mla_w_ukv,
           mla_qk_norm_q, mla_qk_norm_k, diff_q_norm, diff_k_norm, diff_lambda_q1, diff_lambda_k1,
           diff_lambda_q2, diff_lambda_k2, diff_subln, w_out, ffn_norm, peer_w_query, peer_sub_keys,
           peer_u, peer_v):
    b, s, d = x.shape
    assert attn_norm.shape[0] == 1, "meta rows are only used as keys: single layer"
    lambda_init = 0.8 - 0.6 * math.exp(-0.3 * 0)
    tq = min(ATTN_TILE, s)
    assert s % tq == 0

    wi = w_in[0]
    s1, s2, s3 = MLA_Q_RANK, MLA_Q_RANK + MLA_KV_RANK, MLA_Q_RANK + MLA_KV_RANK + MLA_ROPE
    wkr = jnp.pad(wi[:, s2:s3], ((0, 0), (MLA_NOPE, LANES - MLA_QK)))
    w_in_arr = jnp.concatenate([wi[:, :s2], wkr, wi[:, s3:]], axis=1).astype(BF16)
    wuq = _pad_lanes(mla_w_uq[0].reshape(MLA_Q_RANK, MLA_HEADS, MLA_QK), LANES).reshape(MLA_Q_RANK, -1).astype(BF16)
    wukv = mla_w_ukv[0].reshape(MLA_KV_RANK, MLA_HEADS, MLA_NOPE + MLA_V)
    wk = _pad_lanes(wukv[..., :MLA_NOPE], LANES).reshape(MLA_KV_RANK, -1)
    wv = _pad_lanes(wukv[..., MLA_NOPE:], LANES).reshape(MLA_KV_RANK, -1)
    wkv = jnp.concatenate([wk, wv], axis=1).astype(BF16)
    row = lambda a: a.reshape(1, -1).astype(F32)
    gq = _pad_lanes(row(mla_qk_norm_q[0]), LANES)
    gk = _pad_lanes(row(mla_qk_norm_k[0]), LANES)
    gdq = jnp.tile(row(diff_q_norm[0]), (1, 2))
    gdk = jnp.tile(row(diff_k_norm[0]), (1, 2))
    wts = (row(attn_norm[0]), w_in_arr, row(mla_q_norm[0]), wuq, row(mla_kv_norm[0]), wkv, gq, gk, gdq, gdk)

    pos_real = jnp.arange(N_META, N_META + s, dtype=jnp.int32)
    qm, km, vm, qd, kd, vd = _project(x, _rope_tables(pos_real), wts, PROJ_ROWS)
    pos_meta = jnp.arange(N_META, dtype=jnp.int32)
    _, km_m, vm_m, _, kd_m, vd_m = _project(meta_tokens[None].astype(x.dtype), _rope_tables(pos_meta), wts, N_META)
    padk = lambda a: jnp.pad(a[0], ((0, 0), (0, LANES - N_META), (0, 0)))

    mmeta, mdiag = (jnp.asarray(a) for a in _mask_tiles(tq))
    y_mla = _flash(qm[:, :, None], km, vm, padk(km_m), padk(vm_m), mmeta, mdiag, None, (),
                   diff=False, tq=tq, lambda_init=lambda_init)
    bmeta, bdiag, bprev = _bias_tiles(rel_bias, tq)
    extra = (row(diff_lambda_q1[0]), row(diff_lambda_k1[0]), row(diff_lambda_q2[0]), row(diff_lambda_k2[0]),
             row(diff_subln[0]))
    y_diff = _flash(qd, kd, vd, padk(kd_m), padk(vd_m), bmeta, bdiag, bprev, extra,
                    diff=True, tq=tq, lambda_init=lambda_init)

    t = b * s
    wo = w_out[0].astype(BF16)
    n_mla = MLA_HEADS * MLA_V
    wqt = peer_w_query[0].T.astype(BF16)
    sk = peer_sub_keys[0].reshape(PEER_HEADS * 2, PEER_KEYS, PEER_SUB).astype(BF16)
    h1, n2t, ga, gp, gnb, gr = _front(x.reshape(t, d), y_mla.reshape(t, -1), y_diff.reshape(t, -1),
                                      wo[:n_mla], wo[n_mla:], row(ffn_norm[0]), wqt, sk, FRONT_ROWS)
    u_bf = peer_u[0].astype(BF16)
    vt_bf = peer_v[0].T.astype(BF16)
    out = _dense(n2t, u_bf, vt_bf, ga, gp, gnb, gr, h1, DENSE_TOKENS, DENSE_EXPERTS)
    return out.reshape(b, s, d)
```

```python
import functools
import math

import numpy as np
import jax
import jax.numpy as jnp
from jax import lax
from jax.experimental import pallas as pl
from jax.experimental.pallas import tpu as pltpu

F32 = jnp.float32
BF16 = jnp.bfloat16

N_META = 16
EPS = 1e-6
LANES = 128
SUBLANES = 8
PACKED_ROWS = 16

MLA_HEADS = 8
MLA_Q_RANK = 256
MLA_KV_RANK = 256
MLA_NOPE = 64
MLA_ROPE = 32
MLA_V = 64
MLA_QK = MLA_NOPE + MLA_ROPE
ROPE_THETA = 10000.0

DIFF_HEADS = 4
DIFF_HD = 64
DIFF_V = 2 * DIFF_HD

NUM_BUCKETS = 32
MAX_DISTANCE = 128

PEER_HEADS = 8
PEER_KEYS = 128
PEER_TOPK = 16
PEER_SUB = 128

NEG = -1e30

PROJ_ROWS = 256
ATTN_TILE = 512
ATTN_CHAIN_ROWS = 256
LOG2E = math.log2(math.e)
FRONT_ROWS = 256
DENSE_TOKENS = 512
DENSE_EXPERTS = 1024
DENSE_PIECE = 256
GATE_ROWS = 64
VMEM_LIMIT = 56 * 1024 * 1024


def _rms(x, g):
    return x * lax.rsqrt(jnp.mean(x * x, axis=-1, keepdims=True) + EPS) * g


def _dot(a, b):
    return jnp.dot(a, b, preferred_element_type=F32)


def _lane_tile(a, n):
    return a if n == 1 else jnp.concatenate([a] * n, axis=1)


def _dot_nt(a, b):
    return lax.dot_general(a, b, (((1,), (1,)), ((), ())), preferred_element_type=F32)


def _proj_kernel(x_ref, g1_ref, win_ref, gcq_ref, wuq_ref, gckv_ref, wkv_ref, gq_ref, gk_ref,
                 gdq_ref, gdk_ref, cos_ref, sina_ref, sinb_ref,
                 qm_ref, km_ref, vm_ref, qd_ref, kd_ref, vd_ref, *, scale_m, scale_d):
    x = x_ref[0]
    n = _rms(x, g1_ref[...])
    proj = _dot(n.astype(BF16), win_ref[...])
    cq = proj[:, 0:256]
    ckv = proj[:, 256:512]
    krp = proj[:, 512:640]
    dq = proj[:, 640:1152]
    dk = proj[:, 1152:1664]
    dv = proj[:, 1664:2176]
    q_raw = _dot(_rms(cq, gcq_ref[...]).astype(BF16), wuq_ref[...])
    kv = _dot(_rms(ckv, gckv_ref[...]).astype(BF16), wkv_ref[...])
    cos = cos_ref[...]
    sina = sina_ref[...]
    sinb = sinb_ref[...]
    gq = gq_ref[...]
    gk = gk_ref[...]

    def rope(t):
        return t * cos + pltpu.roll(t, LANES - 16, 1) * sina + pltpu.roll(t, 16, 1) * sinb

    inv_qk = 1.0 / MLA_QK
    for h in range(MLA_HEADS):
        sl = slice(h * LANES, (h + 1) * LANES)
        qh = q_raw[:, sl]
        rq = lax.rsqrt(jnp.sum(qh * qh, axis=-1, keepdims=True) * inv_qk + EPS)
        qm_ref[0, h] = (rope(qh * rq * gq) * scale_m).astype(BF16)
        kh = kv[:, sl] + krp
        rk = lax.rsqrt(jnp.sum(kh * kh, axis=-1, keepdims=True) * inv_qk + EPS)
        km_ref[0, h] = rope(kh * rk * gk).astype(BF16)
    lo_v = lax.broadcasted_iota(jnp.int32, (1, LANES), 1) < MLA_V
    for h in range(MLA_HEADS):
        vm_ref[0, h] = jnp.where(lo_v, kv[:, 1024 + h * LANES:1024 + (h + 1) * LANES], 1.0).astype(BF16)

    lo = lax.broadcasted_iota(jnp.int32, (1, LANES), 1) < DIFF_HD
    inv_hd = 1.0 / DIFF_HD

    def halfnorm(t, g):
        sq = t * t
        s_lo = jnp.sum(jnp.where(lo, sq, 0.0), axis=-1, keepdims=True)
        s_hi = jnp.sum(jnp.where(lo, 0.0, sq), axis=-1, keepdims=True)
        r = jnp.where(lo, lax.rsqrt(s_lo * inv_hd + EPS), lax.rsqrt(s_hi * inv_hd + EPS))
        return t * r * g

    gdq = gdq_ref[...]
    gdk = gdk_ref[...]
    for h in range(DIFF_HEADS):
        sl = slice(h * LANES, (h + 1) * LANES)
        qn = halfnorm(dq[:, sl], gdq) * scale_d
        qd_ref[0, h, 0] = jnp.where(lo, qn, 0.0).astype(BF16)
        qd_ref[0, h, 1] = jnp.where(lo, 0.0, qn).astype(BF16)
        kd_ref[0, h] = halfnorm(dk[:, sl], gdk).astype(BF16)
        vd_ref[0, h, :, 0:LANES] = dv[:, sl].astype(BF16)
        vd_ref[0, h, :, LANES:2 * LANES] = jnp.ones_like(dv[:, sl]).astype(BF16)


def _project(x3, tabs, wts, rows):
    b, s, d = x3.shape
    rows = min(rows, s)
    assert s % rows == 0
    grid = (b, s // rows)
    full = lambda a: pl.BlockSpec(a.shape, lambda i, j: (0,) * a.ndim)
    tab_spec = pl.BlockSpec((rows, LANES), lambda i, j: (j, 0))
    in_specs = [pl.BlockSpec((1, rows, d), lambda i, j: (i, j, 0))] + [full(w) for w in wts] + [tab_spec] * 3
    hs = lambda nh: pl.BlockSpec((1, nh, rows, LANES), lambda i, j: (i, 0, j, 0))
    out_shape = (
        jax.ShapeDtypeStruct((b, MLA_HEADS, s, LANES), BF16),
        jax.ShapeDtypeStruct((b, MLA_HEADS, s, LANES), BF16),
        jax.ShapeDtypeStruct((b, MLA_HEADS, s, LANES), BF16),
        jax.ShapeDtypeStruct((b, DIFF_HEADS, 2, s, LANES), BF16),
        jax.ShapeDtypeStruct((b, DIFF_HEADS, s, LANES), BF16),
        jax.ShapeDtypeStruct((b, DIFF_HEADS, s, 2 * LANES), BF16),
    )
    out_specs = (hs(MLA_HEADS), hs(MLA_HEADS), hs(MLA_HEADS),
                 pl.BlockSpec((1, DIFF_HEADS, 2, rows, LANES), lambda i, j: (i, 0, 0, j, 0)),
                 hs(DIFF_HEADS),
                 pl.BlockSpec((1, DIFF_HEADS, rows, 2 * LANES), lambda i, j: (i, 0, j, 0)))
    kern = functools.partial(_proj_kernel, scale_m=MLA_QK ** -0.5 * LOG2E, scale_d=DIFF_HD ** -0.5 * LOG2E)
    return pl.pallas_call(
        kern, grid=grid, in_specs=in_specs, out_specs=out_specs, out_shape=out_shape,
        compiler_params=pltpu.CompilerParams(dimension_semantics=("arbitrary", "arbitrary"),
                                             vmem_limit_bytes=VMEM_LIMIT),
        name="proj",
    )(x3, *wts, *tabs)


def _flash_kernel(*refs, hp, nm, tq, rc, diff, lambda_init):
    if diff:
        (q_ref, k_ref, v_ref, kmeta_ref, vmeta_ref, bmeta_ref, bdiag_ref, bprev_ref,
         lq1_ref, lk1_ref, lq2_ref, lk2_ref, subln_ref, o_ref, m_ref, acc_ref) = refs
    else:
        q_ref, k_ref, v_ref, kmeta_ref, vmeta_ref, bmeta_ref, bdiag_ref, o_ref, m_ref, acc_ref = refs
        bprev_ref = None
    qi = pl.program_id(2)
    nchunk = tq // rc
    nv = acc_ref.shape[1]
    chains = [(hh, mm, c) for hh in range(hp) for mm in range(nm) for c in range(nchunk)]
    qs = [q_ref[0, hh, mm, c * rc:(c + 1) * rc, :] for (hh, mm, c) in chains]

    def update(i, s, vt, first=False):
        rows = slice(i * rc, (i + 1) * rc)
        m_cur = jnp.max(s, axis=-1, keepdims=True)
        if first:
            m_new = jnp.broadcast_to(m_cur, (rc, LANES))
        else:
            m_prev = m_ref[rows]
            m_new = jnp.maximum(m_prev, m_cur)
        p = jnp.exp2((s - _lane_tile(m_new, s.shape[1] // LANES)).astype(BF16))
        pv = _dot(p, vt)
        if first:
            acc_ref[rows] = pv
        else:
            alpha = jnp.exp2(m_prev - m_new)
            acc_ref[rows] = _lane_tile(alpha, nv // LANES) * acc_ref[rows] + pv
        m_ref[rows] = m_new

    for i, (hh, mm, c) in enumerate(chains):
        s = _dot_nt(qs[i], kmeta_ref[hh]) + bmeta_ref[0, 0, c * rc:(c + 1) * rc, :]
        update(i, s, vmeta_ref[hh], first=True)

    def step(j, bias_ref=None):
        off = pl.multiple_of(j * tq, tq)
        scores = []
        for i, (hh, mm, c) in enumerate(chains):
            s = _dot_nt(qs[i], k_ref[0, hh, pl.ds(off, tq), :])
            if bias_ref is not None:
                s = s + bias_ref[0, c * rc:(c + 1) * rc, :]
            scores.append(s)
        for i, (hh, mm, c) in enumerate(chains):
            update(i, scores[i], v_ref[0, hh, pl.ds(off, tq), :])

    def plain(j, carry):
        step(j)
        return carry

    if diff:
        lax.fori_loop(0, jnp.maximum(qi - 1, 0), plain, 0)
        pl.when(qi > 0)(lambda: step(qi - 1, bprev_ref))
    else:
        lax.fori_loop(0, qi, plain, 0)
    step(qi, bdiag_ref)

    def normalized(i):
        acc = acc_ref[i * rc:(i + 1) * rc]
        return acc[:, :LANES] / acc[:, nv - 1:nv]

    outs = {ch: normalized(i) for i, ch in enumerate(chains)}
    if diff:
        lam = (jnp.exp(jnp.sum(lq1_ref[...] * lk1_ref[...], axis=-1, keepdims=True))
               - jnp.exp(jnp.sum(lq2_ref[...] * lk2_ref[...], axis=-1, keepdims=True)) + lambda_init)
        for c in range(nchunk):
            d = outs[(0, 0, c)] - lam * outs[(0, 1, c)]
            y = _rms(d, subln_ref[...]) * (1.0 - lambda_init)
            o_ref[0, c * rc:(c + 1) * rc, :] = y.astype(BF16)
    else:
        lo = lax.broadcasted_iota(jnp.int32, (1, LANES), 1) < MLA_V
        for c in range(nchunk):
            pair = jnp.where(lo, outs[(0, 0, c)], pltpu.roll(outs[(1, 0, c)], MLA_V, 1))
            o_ref[0, c * rc:(c + 1) * rc, :] = pair.astype(BF16)


def _flash(q, k, v, kmeta, vmeta, bmeta, bdiag, bprev, extra, *, diff, tq, lambda_init):
    b, g, nm, s, _ = q.shape
    nv = v.shape[-1]
    hp = 1 if diff else 2
    ngrid = g // hp
    nq = s // tq
    rows = tq
    rc = min(ATTN_CHAIN_ROWS, tq)
    grid = (b, ngrid, nq)
    gsel = (lambda gi: gi) if diff else (lambda gi: 0)
    in_specs = [
        pl.BlockSpec((1, hp, nm, tq, LANES), lambda bi, gi, qi: (bi, gi, 0, qi, 0)),
        pl.BlockSpec((1, hp, s, LANES), lambda bi, gi, qi: (bi, gi, 0, 0)),
        pl.BlockSpec((1, hp, s, nv), lambda bi, gi, qi: (bi, gi, 0, 0)),
        pl.BlockSpec((hp, LANES, LANES), lambda bi, gi, qi: (gi, 0, 0)),
        pl.BlockSpec((hp, LANES, nv), lambda bi, gi, qi: (gi, 0, 0)),
        pl.BlockSpec((1, 1, rows, LANES), lambda bi, gi, qi: (jnp.minimum(qi, 1), gsel(gi), 0, 0)),
        pl.BlockSpec((1, rows, tq), lambda bi, gi, qi: (gsel(gi), 0, 0)),
    ]
    args = [q, k, v, kmeta, vmeta, bmeta, bdiag]
    if diff:
        in_specs.append(pl.BlockSpec((1, rows, tq), lambda bi, gi, qi: (gi, 0, 0)))
        args.append(bprev)
        for e in extra:
            in_specs.append(pl.BlockSpec(e.shape, lambda bi, gi, qi: (0, 0)))
            args.append(e)
    kern = functools.partial(_flash_kernel, hp=hp, nm=nm, tq=tq, rc=rc, diff=diff, lambda_init=lambda_init)
    return pl.pallas_call(
        kern, grid=grid, in_specs=in_specs,
        out_specs=pl.BlockSpec((1, tq, LANES), lambda bi, gi, qi: (bi, qi, gi)),
        out_shape=jax.ShapeDtypeStruct((b, s, ngrid * LANES), BF16),
        scratch_shapes=[pltpu.VMEM((hp * nm * tq, LANES), F32), pltpu.VMEM((hp * nm * tq, nv), F32)],
        compiler_params=pltpu.CompilerParams(dimension_semantics=("arbitrary",) * 3,
                                             vmem_limit_bytes=VMEM_LIMIT),
        name="flash_diff" if diff else "flash_mla",
    )(*args)


def _top_values(s, n, row_ref=None):
    vals = []
    for i in range(n):
        m = jnp.max(s, axis=0, keepdims=True)
        vals.append(m)
        if row_ref is not None and i < PEER_TOPK:
            row_ref[i:i + 1, :] = m
        if i + 1 < n:
            s = jnp.where(s == m, -jnp.inf, s)
    return vals


def _front_kernel(x_ref, ym_ref, yd_ref, woa_ref, wob_ref, g2_ref, wqt_ref, sk_ref,
                  h1_ref, n2t_ref, rho_ref, p_ref, r_ref, v1s_ref, v2s_ref, cand_ref):
    h1 = x_ref[...] + _dot(ym_ref[...], woa_ref[...]) + _dot(yd_ref[...], wob_ref[...])
    h1_ref[...] = h1
    n2t = _rms(h1, g2_ref[...]).T.astype(BF16)
    n2t_ref[...] = n2t
    qpt = _dot(wqt_ref[...], n2t).astype(BF16)
    k = PEER_TOPK
    for h in range(PEER_HEADS):
        s1 = _dot(sk_ref[2 * h], qpt[(2 * h) * PEER_SUB:(2 * h + 1) * PEER_SUB])
        s2 = _dot(sk_ref[2 * h + 1], qpt[(2 * h + 1) * PEER_SUB:(2 * h + 2) * PEER_SUB])
        v1 = _top_values(s1, k + 1, v1s_ref)
        v2 = _top_values(s2, k + 1, v2s_ref)
        v2lo = v2s_ref[0:SUBLANES, :]
        row = lax.broadcasted_iota(jnp.int32, v2lo.shape, 0)
        cand_ref[0:k, :] = v1[0] + v2s_ref[...]
        cand_ref[k:k + SUBLANES, :] = v1[1] + v2lo
        for a in range(2, SUBLANES):
            lo = k + (a - 1) * SUBLANES
            cand_ref[lo:lo + SUBLANES, :] = jnp.where(row < (k + 1) // (a + 1), v1[a] + v2lo, -jnp.inf)
        cand_ref[k + 7 * SUBLANES:k + 8 * SUBLANES, :] = v1s_ref[SUBLANES:k, :] + v2[0]
        c = _top_values(cand_ref[...], k + 1)
        c17 = jnp.maximum(c[k], jnp.maximum(v1[k] + v2[0], v1[0] + v2[k]))
        tau = 0.5 * (c[k - 1] + c17)
        z = jnp.exp(c[0] - c[0])
        for i in range(1, k):
            z = z + jnp.exp(c[i] - c[0])
        rho = jnp.full(s1.shape, jnp.inf, F32)
        for l in range(k):
            rho = jnp.where(s1 + v2[l] >= tau, jnp.exp(v2[l] - v2[0]), rho)
        rho_ref[h] = rho
        p_ref[h] = jnp.exp(s1 - v1[0]) * (0.5 / z)
        r_ref[h] = jnp.exp(s2 - v2[0])


def _front(x2, ym, yd, woa, wob, g2, wqt, sk, rows):
    t, d = x2.shape
    rows = min(rows, t)
    grid = (t // rows,)
    full = lambda a: pl.BlockSpec(a.shape, lambda i: (0,) * a.ndim)
    gate = jax.ShapeDtypeStruct((PEER_HEADS, PEER_KEYS, t), F32)
    gate_spec = pl.BlockSpec((PEER_HEADS, PEER_KEYS, rows), lambda i: (0, 0, i))
    return pl.pallas_call(
        _front_kernel, grid=grid,
        in_specs=[pl.BlockSpec((rows, d), lambda i: (i, 0)),
                  pl.BlockSpec((rows, ym.shape[1]), lambda i: (i, 0)),
                  pl.BlockSpec((rows, yd.shape[1]), lambda i: (i, 0)),
                  full(woa), full(wob), full(g2), full(wqt), full(sk)],
        out_specs=(pl.BlockSpec((rows, d), lambda i: (i, 0)),
                   pl.BlockSpec((d, rows), lambda i: (0, i)),
                   gate_spec, gate_spec, gate_spec),
        out_shape=(jax.ShapeDtypeStruct((t, d), F32), jax.ShapeDtypeStruct((d, t), BF16),
                   gate, gate, gate),
        scratch_shapes=[pltpu.VMEM((PEER_TOPK, rows), F32), pltpu.VMEM((PEER_TOPK, rows), F32),
                        pltpu.VMEM((PEER_TOPK + 8 * SUBLANES, rows), F32)],
        compiler_params=pltpu.CompilerParams(dimension_semantics=("arbitrary",),
                                             vmem_limit_bytes=VMEM_LIMIT),
        name="front",
    )(x2, ym, yd, woa, wob, g2, wqt, sk)


def _dense_kernel(n2t_ref, u_ref, vt_ref, vtp_ref, rho_ref, p_ref, r_ref, h1_ref, o_ref,
                  acc_ref, w_ref, pre_ref, brho_ref, bp_ref, wc_ref):
    e = pl.program_id(1)
    n_tok = n2t_ref.shape[1]
    n_exp = u_ref.shape[0]
    piece = DENSE_PIECE
    n_piece = n_exp // piece
    first_keys = piece // PEER_KEYS
    n_i = n_exp // PEER_KEYS
    n_tc = n_tok // LANES
    assert n_tc % 2 == 0
    first, second = range(0, n_tc // 2), range(n_tc // 2, n_tc)
    cols_a = slice(0, n_tok // 2)
    cols_b = slice(n_tok // 2, n_tok)

    @pl.when(e == 0)
    def _():
        acc_ref[...] = jnp.zeros_like(acc_ref)
        wc_ref[...] = jnp.zeros_like(wc_ref)

    for h in range(PEER_HEADS):
        for ii in range(n_i):
            brho_ref[h * n_i + ii] = jnp.broadcast_to(rho_ref[h, ii:ii + 1, :], (SUBLANES, n_tok))
            bp_ref[h * n_i + ii] = jnp.broadcast_to(p_ref[h, ii:ii + 1, :], (SUBLANES, n_tok))

    def pre_matmuls(cols):
        for pc in range(n_piece):
            rows = slice(pc * piece, (pc + 1) * piece)
            pre_ref[rows, cols] = _dot(u_ref[rows, :], n2t_ref[:, cols])

    pre_matmuls(cols_a)
    acc_ref[:, cols_b] += _dot(vtp_ref[...], wc_ref[...])
    pre_matmuls(cols_b)

    def gate(pc, tcs):
        i0 = pc * first_keys
        nvr = GATE_ROWS // SUBLANES
        for tc in tcs:
            cs = slice(tc * LANES, (tc + 1) * LANES)
            for jt in range(PEER_KEYS // GATE_ROWS):
                js = slice(jt * GATE_ROWS, (jt + 1) * GATE_ROWS)
                g = [None] * first_keys
                for h in range(PEER_HEADS):
                    r = r_ref[h, js, cs].reshape(nvr, SUBLANES, LANES)
                    for k in range(first_keys):
                        rho = brho_ref[h * n_i + i0 + k, :, cs][None]
                        p = bp_ref[h * n_i + i0 + k, :, cs][None]
                        term = jnp.where(r >= rho, p * r, 0.0)
                        g[k] = term if g[k] is None else g[k] + term
                for k in range(first_keys):
                    row0 = (i0 + k) * PEER_KEYS + jt * GATE_ROWS
                    pre = pre_ref[row0:row0 + GATE_ROWS, cs]
                    act = pre * (1.0 + lax.erf(pre * math.sqrt(0.5)))
                    w_ref[row0:row0 + GATE_ROWS, cs] = (g[k].reshape(GATE_ROWS, LANES) * act).astype(BF16)

    for pc in range(n_piece):
        gate(pc, first)
    acc_ref[:, cols_a] += _dot(vt_ref[...], w_ref[:, cols_a])
    for pc in range(n_piece):
        gate(pc, second)
    wc_ref[...] = w_ref[:, cols_b]

    @pl.when(e == pl.num_programs(1) - 1)
    def _():
        acc_ref[:, cols_b] += _dot(vt_ref[...], wc_ref[...])
        o_ref[...] = h1_ref[...] + acc_ref[...].T


def _dense(n2t, u, vt, rho, p, r, h1, tokens, experts):
    d, t = n2t.shape
    ne = u.shape[0]
    tokens = min(tokens, t)
    grid = (t // tokens, ne // experts)
    n_i = experts // PEER_KEYS
    return pl.pallas_call(
        _dense_kernel, grid=grid,
        in_specs=[pl.BlockSpec((d, tokens), lambda ti, ei: (0, ti)),
                  pl.BlockSpec((experts, d), lambda ti, ei: (ei, 0)),
                  pl.BlockSpec((d, experts), lambda ti, ei: (0, ei)),
                  pl.BlockSpec((d, experts), lambda ti, ei: (0, jnp.maximum(ei - 1, 0))),
                  pl.BlockSpec((PEER_HEADS, n_i, tokens), lambda ti, ei: (0, ei, ti)),
                  pl.BlockSpec((PEER_HEADS, n_i, tokens), lambda ti, ei: (0, ei, ti)),
                  pl.BlockSpec((PEER_HEADS, PEER_KEYS, tokens), lambda ti, ei: (0, 0, ti)),
                  pl.BlockSpec((tokens, d), lambda ti, ei: (ti, 0))],
        out_specs=pl.BlockSpec((tokens, d), lambda ti, ei: (ti, 0)),
        out_shape=jax.ShapeDtypeStruct((t, d), F32),
        scratch_shapes=[pltpu.VMEM((d, tokens), F32),
                        pltpu.VMEM((experts, tokens), BF16),
                        pltpu.VMEM((experts, tokens), F32),
                        pltpu.VMEM((PEER_HEADS * n_i, SUBLANES, tokens), F32),
                        pltpu.VMEM((PEER_HEADS * n_i, SUBLANES, tokens), F32),
                        pltpu.VMEM((experts, tokens // 2), BF16)],
        compiler_params=pltpu.CompilerParams(dimension_semantics=("arbitrary", "arbitrary"),
                                             vmem_limit_bytes=VMEM_LIMIT),
        name="dense",
    )(n2t, u, vt, vt, rho, p, r, h1)


def _rope_tables(pos):
    half = MLA_ROPE // 2
    inv_freq = ROPE_THETA ** (-jnp.arange(half, dtype=F32) / half)
    ang = pos.astype(F32)[:, None] * inv_freq[None, :]
    cos, sin = jnp.cos(ang), jnp.sin(ang)
    n = pos.shape[0]
    one = jnp.ones((n, MLA_NOPE), F32)
    zero = jnp.zeros((n, MLA_NOPE), F32)
    tail1 = jnp.ones((n, LANES - MLA_QK), F32)
    tail0 = jnp.zeros((n, LANES - MLA_QK), F32)
    zh = jnp.zeros((n, half), F32)
    cos_t = jnp.concatenate([one, cos, cos, tail1], axis=1)
    sina = jnp.concatenate([zero, -sin, zh, tail0], axis=1)
    sinb = jnp.concatenate([zero, zh, sin, tail0], axis=1)
    return cos_t, sina, sinb


def _bucket_np(dist):
    n = np.maximum(dist, 0)
    max_exact = NUM_BUCKETS // 2
    nf = np.maximum(n, 1).astype(np.float32)
    large = max_exact + (np.log(nf / np.float32(max_exact)) / np.float32(math.log(MAX_DISTANCE / max_exact))
                         * np.float32(NUM_BUCKETS - max_exact)).astype(np.int32)
    large = np.minimum(large, NUM_BUCKETS - 1)
    return np.where(n < max_exact, n, large)


def _bias_tiles(rel_bias, tq):
    far = int(np.min(np.nonzero(_bucket_np(np.arange(4 * MAX_DISTANCE)) == NUM_BUCKETS - 1)[0]))
    assert np.all(_bucket_np(np.arange(far, 8 * MAX_DISTANCE)) == NUM_BUCKETS - 1)
    if tq < far:
        raise NotImplementedError("attention tile smaller than the relative-bias window")
    table = rel_bias.astype(F32)
    shifted = (table - table[NUM_BUCKETS - 1][None, :]) * LOG2E

    def toeplitz(n_rows, n_cols, off):
        period = n_rows + n_cols - 1
        dist = np.arange(period) - (n_cols - 1) + off
        vals = jnp.where((dist >= 0)[:, None], shifted[_bucket_np(np.maximum(dist, 0))], NEG)
        u = vals[::-1].T
        flat = jnp.tile(u, (1, n_rows + 1))[:, :n_rows * (period + 1)]
        return flat.reshape(-1, n_rows, period + 1)[:, ::-1, :n_cols]

    diag = toeplitz(tq, tq, 0)
    prev = toeplitz(tq, tq, tq)
    pad = jnp.asarray(_mask_tiles(tq)[0][:, 0])
    nh = rel_bias.shape[1]
    meta0 = jnp.where(pad[0][None] < 0.0, NEG, toeplitz(tq, LANES, N_META))
    meta_far = jnp.broadcast_to(pad[1][None], (nh, tq, LANES))
    return jnp.stack([meta0, meta_far]), diag, prev


def _mask_tiles(tq):
    r = np.arange(tq)[:, None]
    c = np.arange(tq)[None, :]
    diag = np.where(r >= c, 0.0, NEG).astype(np.float32)[None]
    cm = np.arange(LANES)[None, :]
    meta = np.broadcast_to(np.where(cm < N_META, 0.0, NEG).astype(np.float32), (tq, LANES))
    return np.stack([meta, meta])[:, None], diag


def _pad_lanes(a, width):
    return jnp.pad(a, [(0, 0)] * (a.ndim - 1) + [(0, width - a.shape[-1])])


def kernel(x, meta_tokens, rel_bias, attn_norm, w_in, mla_q_norm, mla_w_uq, mla_kv_norm, mla_w_ukv,
           mla_qk_norm_q, mla_qk_norm_k, diff_q_norm, diff_k_norm, diff_lambda_q1, diff_lambda_k1,
           diff_lambda_q2, diff_lambda_k2, diff_subln, w_out, ffn_norm, peer_w_query, peer_sub_keys,
           peer_u, peer_v):
    b, s, d = x.shape
    assert attn_norm.shape[0] == 1, "meta rows are only used as keys: single layer"
    lambda_init = 0.8 - 0.6 * math.exp(-0.3 * 0)
    tq = min(ATTN_TILE, s)
    assert s % tq == 0

    wi = w_in[0]
    s1, s2, s3 = MLA_Q_RANK, MLA_Q_RANK + MLA_KV_RANK, MLA_Q_RANK + MLA_KV_RANK + MLA_ROPE
    wkr = jnp.pad(wi[:, s2:s3], ((0, 0), (MLA_NOPE, LANES - MLA_QK)))
    w_in_arr = jnp.concatenate([wi[:, :s2], wkr, wi[:, s3:]], axis=1).astype(BF16)
    wuq = _pad_lanes(mla_w_uq[0].reshape(MLA_Q_RANK, MLA_HEADS, MLA_QK), LANES).reshape(MLA_Q_RANK, -1).astype(BF16)
    wukv = mla_w_ukv[0].reshape(MLA_KV_RANK, MLA_HEADS, MLA_NOPE + MLA_V)
    wk = _pad_lanes(wukv[..., :MLA_NOPE], LANES).reshape(MLA_KV_RANK, -1)
    wv = _pad_lanes(wukv[..., MLA_NOPE:], LANES).reshape(MLA_KV_RANK, -1)
    wkv = jnp.concatenate([wk, wv], axis=1).astype(BF16)
    row = lambda a: a.reshape(1, -1).astype(F32)
    gq = _pad_lanes(row(mla_qk_norm_q[0]), LANES)
    gk = _pad_lanes(row(mla_qk_norm_k[0]), LANES)
    gdq = jnp.tile(row(diff_q_norm[0]), (1, 2))
    gdk = jnp.tile(row(diff_k_norm[0]), (1, 2))
    wts = (row(attn_norm[0]), w_in_arr, row(mla_q_norm[0]), wuq, row(mla_kv_norm[0]), wkv, gq, gk, gdq, gdk)

    pos_real = jnp.arange(N_META, N_META + s, dtype=jnp.int32)
    qm, km, vm, qd, kd, vd = _project(x, _rope_tables(pos_real), wts, PROJ_ROWS)
    pos_meta = jnp.arange(N_META, dtype=jnp.int32)
    _, km_m, vm_m, _, kd_m, vd_m = _project(meta_tokens[None].astype(x.dtype), _rope_tables(pos_meta), wts, N_META)
    padk = lambda a: jnp.pad(a[0], ((0, 0), (0, LANES - N_META), (0, 0)))

    mmeta, mdiag = (jnp.asarray(a) for a in _mask_tiles(tq))
    y_mla = _flash(qm[:, :, None], km, vm, padk(km_m), padk(vm_m), mmeta, mdiag, None, (),
                   diff=False, tq=tq, lambda_init=lambda_init)
    bmeta, bdiag, bprev = _bias_tiles(rel_bias, tq)
    extra = (row(diff_lambda_q1[0]), row(diff_lambda_k1[0]), row(diff_lambda_q2[0]), row(diff_lambda_k2[0]),
             row(diff_subln[0]))
    y_diff = _flash(qd, kd, vd, padk(kd_m), padk(vd_m), bmeta, bdiag, bprev, extra,
                    diff=True, tq=tq, lambda_init=lambda_init)

    t = b * s
    wo = w_out[0].astype(BF16)
    n_mla = MLA_HEADS * MLA_V
    wqt = peer_w_query[0].T.astype(BF16)
    sk = peer_sub_keys[0].reshape(PEER_HEADS * 2, PEER_KEYS, PEER_SUB).astype(BF16)
    h1, n2t, g_rho, g_p, g_r = _front(x.reshape(t, d), y_mla.reshape(t, -1), y_diff.reshape(t, -1),
                                      wo[:n_mla], wo[n_mla:], row(ffn_norm[0]), wqt, sk, FRONT_ROWS)
    u_bf = peer_u[0].astype(BF16)
    vt_bf = peer_v[0].T.astype(BF16)
    out = _dense(n2t, u_bf, vt_bf, g_rho, g_p, g_r, h1, DENSE_TOKENS, DENSE_EXPERTS)
    return out.reshape(b, s, d)
```

```python
import functools
import math

import numpy as np
import jax
import jax.numpy as jnp
from jax import lax
from jax.experimental import pallas as pl
from jax.experimental.pallas import tpu as pltpu

F32 = jnp.float32
BF16 = jnp.bfloat16

N_META = 16
EPS = 1e-6
LANES = 128
SUBLANES = 8
PACKED_ROWS = 16

MLA_HEADS = 8
MLA_Q_RANK = 256
MLA_KV_RANK = 256
MLA_NOPE = 64
MLA_ROPE = 32
MLA_V = 64
MLA_QK = MLA_NOPE + MLA_ROPE
ROPE_THETA = 10000.0

DIFF_HEADS = 4
DIFF_HD = 64
DIFF_V = 2 * DIFF_HD

NUM_BUCKETS = 32
MAX_DISTANCE = 128

PEER_HEADS = 8
PEER_KEYS = 128
PEER_TOPK = 16
PEER_SUB = 128

NEG = -1e30

PROJ_ROWS = 512
PROJ_SUB_ROWS = 256
ATTN_TILE = 512
ATTN_CHAIN_ROWS = 256
LOG2E = math.log2(math.e)
FRONT_ROWS = 256
DENSE_TOKENS = 512
DENSE_EXPERTS = 1024
DENSE_PIECE = 256
GATE_ROWS = 64
VMEM_LIMIT = 56 * 1024 * 1024


def _rms(x, g):
    return x * lax.rsqrt(jnp.mean(x * x, axis=-1, keepdims=True) + EPS) * g


def _dot(a, b):
    return jnp.dot(a, b, preferred_element_type=F32)


def _lane_tile(a, n):
    return a if n == 1 else jnp.concatenate([a] * n, axis=1)


def _dot_nt(a, b):
    return lax.dot_general(a, b, (((1,), (1,)), ((), ())), preferred_element_type=F32)


def _proj_kernel(x_ref, g1_ref, win_ref, gcq_ref, wuq_ref, gckv_ref, wkv_ref, gq_ref, gk_ref,
                 gdq_ref, gdk_ref, cos_ref, sina_ref, sinb_ref,
                 qm_ref, km_ref, vm_ref, qd_ref, kd_ref, vd_ref, *, scale_m, scale_d):
    rows = x_ref.shape[1]
    sub = min(PROJ_SUB_ROWS, rows)
    subs = [slice(i * sub, (i + 1) * sub) for i in range(rows // sub)]
    ns = [_rms(x_ref[0, sl, :], g1_ref[...]).astype(BF16) for sl in subs]
    projs = [_dot(n, win_ref[...]) for n in ns]
    cqs = [_rms(p[:, 0:256], gcq_ref[...]).astype(BF16) for p in projs]
    ckvs = [_rms(p[:, 256:512], gckv_ref[...]).astype(BF16) for p in projs]
    q_raws = [_dot(c, wuq_ref[...]) for c in cqs]
    kvs = [_dot(c, wkv_ref[...]) for c in ckvs]
    gq = gq_ref[...]
    gk = gk_ref[...]
    gdq = gdq_ref[...]
    gdk = gdk_ref[...]
    lo_v = lax.broadcasted_iota(jnp.int32, (1, LANES), 1) < MLA_V
    lo = lax.broadcasted_iota(jnp.int32, (1, LANES), 1) < DIFF_HD
    inv_qk = 1.0 / MLA_QK
    inv_hd = 1.0 / DIFF_HD

    def halfnorm(t, g):
        sq = t * t
        s_lo = jnp.sum(jnp.where(lo, sq, 0.0), axis=-1, keepdims=True)
        s_hi = jnp.sum(jnp.where(lo, 0.0, sq), axis=-1, keepdims=True)
        r = jnp.where(lo, lax.rsqrt(s_lo * inv_hd + EPS), lax.rsqrt(s_hi * inv_hd + EPS))
        return t * r * g

    for sl, proj, q_raw, kv in zip(subs, projs, q_raws, kvs):
        krp = proj[:, 512:640]
        dq = proj[:, 640:1152]
        dk = proj[:, 1152:1664]
        dv = proj[:, 1664:2176]
        cos = cos_ref[sl, :]
        sina = sina_ref[sl, :]
        sinb = sinb_ref[sl, :]

        def rope(t, cos=cos, sina=sina, sinb=sinb):
            return t * cos + pltpu.roll(t, LANES - 16, 1) * sina + pltpu.roll(t, 16, 1) * sinb

        for h in range(MLA_HEADS):
            hl = slice(h * LANES, (h + 1) * LANES)
            qh = q_raw[:, hl]
            rq = lax.rsqrt(jnp.sum(qh * qh, axis=-1, keepdims=True) * inv_qk + EPS)
            qm_ref[0, h, sl, :] = (rope(qh * rq * gq) * scale_m).astype(BF16)
            kh = kv[:, hl] + krp
            rk = lax.rsqrt(jnp.sum(kh * kh, axis=-1, keepdims=True) * inv_qk + EPS)
            km_ref[0, h, sl, :] = rope(kh * rk * gk).astype(BF16)
            vm_ref[0, h, sl, :] = jnp.where(lo_v, kv[:, 1024 + h * LANES:1024 + (h + 1) * LANES], 1.0).astype(BF16)
        for h in range(DIFF_HEADS):
            hl = slice(h * LANES, (h + 1) * LANES)
            qn = halfnorm(dq[:, hl], gdq) * scale_d
            qd_ref[0, h, 0, sl, :] = jnp.where(lo, qn, 0.0).astype(BF16)
            qd_ref[0, h, 1, sl, :] = jnp.where(lo, 0.0, qn).astype(BF16)
            kd_ref[0, h, sl, :] = halfnorm(dk[:, hl], gdk).astype(BF16)
            vd_ref[0, h, sl, 0:LANES] = dv[:, hl].astype(BF16)
            vd_ref[0, h, sl, LANES:2 * LANES] = jnp.ones_like(dv[:, hl]).astype(BF16)


def _project(x3, tabs, wts, rows):
    b, s, d = x3.shape
    rows = min(rows, s)
    assert s % rows == 0
    grid = (b, s // rows)
    full = lambda a: pl.BlockSpec(a.shape, lambda i, j: (0,) * a.ndim)
    tab_spec = pl.BlockSpec((rows, LANES), lambda i, j: (j, 0))
    in_specs = [pl.BlockSpec((1, rows, d), lambda i, j: (i, j, 0))] + [full(w) for w in wts] + [tab_spec] * 3
    hs = lambda nh: pl.BlockSpec((1, nh, rows, LANES), lambda i, j: (i, 0, j, 0))
    out_shape = (
        jax.ShapeDtypeStruct((b, MLA_HEADS, s, LANES), BF16),
        jax.ShapeDtypeStruct((b, MLA_HEADS, s, LANES), BF16),
        jax.ShapeDtypeStruct((b, MLA_HEADS, s, LANES), BF16),
        jax.ShapeDtypeStruct((b, DIFF_HEADS, 2, s, LANES), BF16),
        jax.ShapeDtypeStruct((b, DIFF_HEADS, s, LANES), BF16),
        jax.ShapeDtypeStruct((b, DIFF_HEADS, s, 2 * LANES), BF16),
    )
    out_specs = (hs(MLA_HEADS), hs(MLA_HEADS), hs(MLA_HEADS),
                 pl.BlockSpec((1, DIFF_HEADS, 2, rows, LANES), lambda i, j: (i, 0, 0, j, 0)),
                 hs(DIFF_HEADS),
                 pl.BlockSpec((1, DIFF_HEADS, rows, 2 * LANES), lambda i, j: (i, 0, j, 0)))
    kern = functools.partial(_proj_kernel, scale_m=MLA_QK ** -0.5 * LOG2E, scale_d=DIFF_HD ** -0.5 * LOG2E)
    return pl.pallas_call(
        kern, grid=grid, in_specs=in_specs, out_specs=out_specs, out_shape=out_shape,
        compiler_params=pltpu.CompilerParams(dimension_semantics=("arbitrary", "arbitrary"),
                                             vmem_limit_bytes=VMEM_LIMIT),
        name="proj",
    )(x3, *wts, *tabs)


def _flash_kernel(*refs, hp, nm, tq, rc, diff, lambda_init):
    if diff:
        (q_ref, k_ref, v_ref, kmeta_ref, vmeta_ref, bmeta_ref, bdiag_ref, bprev_ref,
         lq1_ref, lk1_ref, lq2_ref, lk2_ref, subln_ref, o_ref, m_ref, acc_ref) = refs
    else:
        q_ref, k_ref, v_ref, kmeta_ref, vmeta_ref, bmeta_ref, bdiag_ref, o_ref, m_ref, acc_ref = refs
        bprev_ref = None
    qi = pl.program_id(2)
    nchunk = tq // rc
    nv = acc_ref.shape[1]
    chains = [(hh, mm, c) for hh in range(hp) for mm in range(nm) for c in range(nchunk)]
    qs = [q_ref[0, hh, mm, c * rc:(c + 1) * rc, :] for (hh, mm, c) in chains]

    def update(i, s, vt, first=False):
        rows = slice(i * rc, (i + 1) * rc)
        m_cur = jnp.max(s, axis=-1, keepdims=True)
        if first:
            m_new = jnp.broadcast_to(m_cur, (rc, LANES))
        else:
            m_prev = m_ref[rows]
            m_new = jnp.maximum(m_prev, m_cur)
        p = jnp.exp2((s - _lane_tile(m_new, s.shape[1] // LANES)).astype(BF16))
        pv = _dot(p, vt)
        if first:
            acc_ref[rows] = pv
        else:
            alpha = jnp.exp2(m_prev - m_new)
            acc_ref[rows] = _lane_tile(alpha, nv // LANES) * acc_ref[rows] + pv
        m_ref[rows] = m_new

    for i, (hh, mm, c) in enumerate(chains):
        s = _dot_nt(qs[i], kmeta_ref[hh]) + bmeta_ref[0, 0, c * rc:(c + 1) * rc, :]
        update(i, s, vmeta_ref[hh], first=True)

    def step(j, bias_ref=None):
        off = pl.multiple_of(j * tq, tq)
        scores = []
        for i, (hh, mm, c) in enumerate(chains):
            s = _dot_nt(qs[i], k_ref[0, hh, pl.ds(off, tq), :])
            if bias_ref is not None:
                s = s + bias_ref[0, c * rc:(c + 1) * rc, :]
            scores.append(s)
        for i, (hh, mm, c) in enumerate(chains):
            update(i, scores[i], v_ref[0, hh, pl.ds(off, tq), :])

    def plain(j, carry):
        step(j)
        return carry

    if diff:
        lax.fori_loop(0, jnp.maximum(qi - 1, 0), plain, 0)
        pl.when(qi > 0)(lambda: step(qi - 1, bprev_ref))
    else:
        lax.fori_loop(0, qi, plain, 0)
    step(qi, bdiag_ref)

    def normalized(i):
        acc = acc_ref[i * rc:(i + 1) * rc]
        return acc[:, :LANES] / acc[:, nv - 1:nv]

    outs = {ch: normalized(i) for i, ch in enumerate(chains)}
    if diff:
        lam = (jnp.exp(jnp.sum(lq1_ref[...] * lk1_ref[...], axis=-1, keepdims=True))
               - jnp.exp(jnp.sum(lq2_ref[...] * lk2_ref[...], axis=-1, keepdims=True)) + lambda_init)
        for c in range(nchunk):
            d = outs[(0, 0, c)] - lam * outs[(0, 1, c)]
            y = _rms(d, subln_ref[...]) * (1.0 - lambda_init)
            o_ref[0, c * rc:(c + 1) * rc, :] = y.astype(BF16)
    else:
        lo = lax.broadcasted_iota(jnp.int32, (1, LANES), 1) < MLA_V
        for c in range(nchunk):
            pair = jnp.where(lo, outs[(0, 0, c)], pltpu.roll(outs[(1, 0, c)], MLA_V, 1))
            o_ref[0, c * rc:(c + 1) * rc, :] = pair.astype(BF16)


def _flash(q, k, v, kmeta, vmeta, bmeta, bdiag, bprev, extra, *, diff, tq, lambda_init):
    b, g, nm, s, _ = q.shape
    nv = v.shape[-1]
    hp = 1 if diff else 2
    ngrid = g // hp
    nq = s // tq
    rows = tq
    rc = min(ATTN_CHAIN_ROWS, tq)
    grid = (b, ngrid, nq)
    gsel = (lambda gi: gi) if diff else (lambda gi: 0)
    in_specs = [
        pl.BlockSpec((1, hp, nm, tq, LANES), lambda bi, gi, qi: (bi, gi, 0, qi, 0)),
        pl.BlockSpec((1, hp, s, LANES), lambda bi, gi, qi: (bi, gi, 0, 0)),
        pl.BlockSpec((1, hp, s, nv), lambda bi, gi, qi: (bi, gi, 0, 0)),
        pl.BlockSpec((hp, LANES, LANES), lambda bi, gi, qi: (gi, 0, 0)),
        pl.BlockSpec((hp, LANES, nv), lambda bi, gi, qi: (gi, 0, 0)),
        pl.BlockSpec((1, 1, rows, LANES), lambda bi, gi, qi: (jnp.minimum(qi, 1), gsel(gi), 0, 0)),
        pl.BlockSpec((1, rows, tq), lambda bi, gi, qi: (gsel(gi), 0, 0)),
    ]
    args = [q, k, v, kmeta, vmeta, bmeta, bdiag]
    if diff:
        in_specs.append(pl.BlockSpec((1, rows, tq), lambda bi, gi, qi: (gi, 0, 0)))
        args.append(bprev)
        for e in extra:
            in_specs.append(pl.BlockSpec(e.shape, lambda bi, gi, qi: (0, 0)))
            args.append(e)
    kern = functools.partial(_flash_kernel, hp=hp, nm=nm, tq=tq, rc=rc, diff=diff, lambda_init=lambda_init)
    return pl.pallas_call(
        kern, grid=grid, in_specs=in_specs,
        out_specs=pl.BlockSpec((1, tq, LANES), lambda bi, gi, qi: (bi, qi, gi)),
        out_shape=jax.ShapeDtypeStruct((b, s, ngrid * LANES), BF16),
        scratch_shapes=[pltpu.VMEM((hp * nm * tq, LANES), F32), pltpu.VMEM((hp * nm * tq, nv), F32)],
        compiler_params=pltpu.CompilerParams(dimension_semantics=("arbitrary",) * 3,
                                             vmem_limit_bytes=VMEM_LIMIT),
        name="flash_diff" if diff else "flash_mla",
    )(*args)


def _top_values(s, n, row_ref=None):
    vals = []
    for i in range(n):
        m = jnp.max(s, axis=0, keepdims=True)
        vals.append(m)
        if row_ref is not None and i < PEER_TOPK:
            row_ref[i:i + 1, :] = m
        if i + 1 < n:
            s = jnp.where(s == m, -jnp.inf, s)
    return vals


def _front_kernel(x_ref, ym_ref, yd_ref, woa_ref, wob_ref, g2_ref, wqt_ref, sk_ref,
                  h1_ref, n2t_ref, rho_ref, p_ref, r_ref, v1s_ref, v2s_ref, cand_ref):
    h1 = x_ref[...] + _dot(ym_ref[...], woa_ref[...]) + _dot(yd_ref[...], wob_ref[...])
    h1_ref[...] = h1
    n2t = _rms(h1, g2_ref[...]).T.astype(BF16)
    n2t_ref[...] = n2t
    qpt = _dot(wqt_ref[...], n2t).astype(BF16)
    k = PEER_TOPK
    for h in range(PEER_HEADS):
        s1 = _dot(sk_ref[2 * h], qpt[(2 * h) * PEER_SUB:(2 * h + 1) * PEER_SUB])
        s2 = _dot(sk_ref[2 * h + 1], qpt[(2 * h + 1) * PEER_SUB:(2 * h + 2) * PEER_SUB])
        v1 = _top_values(s1, k + 1, v1s_ref)
        v2 = _top_values(s2, k + 1, v2s_ref)
        v2lo = v2s_ref[0:SUBLANES, :]
        row = lax.broadcasted_iota(jnp.int32, v2lo.shape, 0)
        cand_ref[0:k, :] = v1[0] + v2s_ref[...]
        cand_ref[k:k + SUBLANES, :] = v1[1] + v2lo
        for a in range(2, SUBLANES):
            lo = k + (a - 1) * SUBLANES
            cand_ref[lo:lo + SUBLANES, :] = jnp.where(row < (k + 1) // (a + 1), v1[a] + v2lo, -jnp.inf)
        cand_ref[k + 7 * SUBLANES:k + 8 * SUBLANES, :] = v1s_ref[SUBLANES:k, :] + v2[0]
        c = _top_values(cand_ref[...], k + 1)
        c17 = jnp.maximum(c[k], jnp.maximum(v1[k] + v2[0], v1[0] + v2[k]))
        tau = 0.5 * (c[k - 1] + c17)
        z = jnp.exp(c[0] - c[0])
        for i in range(1, k):
            z = z + jnp.exp(c[i] - c[0])
        rho = jnp.full(s1.shape, jnp.inf, F32)
        for l in range(k):
            rho = jnp.where(s1 + v2[l] >= tau, jnp.exp(v2[l] - v2[0]), rho)
        rho_ref[h] = rho
        p_ref[h] = jnp.exp(s1 - v1[0]) * (0.5 / z)
        r_ref[h] = jnp.exp(s2 - v2[0])


def _front(x2, ym, yd, woa, wob, g2, wqt, sk, rows):
    t, d = x2.shape
    rows = min(rows, t)
    grid = (t // rows,)
    full = lambda a: pl.BlockSpec(a.shape, lambda i: (0,) * a.ndim)
    gate = jax.ShapeDtypeStruct((PEER_HEADS, PEER_KEYS, t), F32)
    gate_spec = pl.BlockSpec((PEER_HEADS, PEER_KEYS, rows), lambda i: (0, 0, i))
    return pl.pallas_call(
        _front_kernel, grid=grid,
        in_specs=[pl.BlockSpec((rows, d), lambda i: (i, 0)),
                  pl.BlockSpec((rows, ym.shape[1]), lambda i: (i, 0)),
                  pl.BlockSpec((rows, yd.shape[1]), lambda i: (i, 0)),
                  full(woa), full(wob), full(g2), full(wqt), full(sk)],
        out_specs=(pl.BlockSpec((rows, d), lambda i: (i, 0)),
                   pl.BlockSpec((d, rows), lambda i: (0, i)),
                   gate_spec, gate_spec, gate_spec),
        out_shape=(jax.ShapeDtypeStruct((t, d), F32), jax.ShapeDtypeStruct((d, t), BF16),
                   gate, gate, gate),
        scratch_shapes=[pltpu.VMEM((PEER_TOPK, rows), F32), pltpu.VMEM((PEER_TOPK, rows), F32),
                        pltpu.VMEM((PEER_TOPK + 8 * SUBLANES, rows), F32)],
        compiler_params=pltpu.CompilerParams(dimension_semantics=("arbitrary",),
                                             vmem_limit_bytes=VMEM_LIMIT),
        name="front",
    )(x2, ym, yd, woa, wob, g2, wqt, sk)


def _dense_kernel(n2t_ref, u_ref, vt_ref, rho_ref, p_ref, r_ref, h1_ref, o_ref,
                  acc_ref, w_ref, pre_ref, brho_ref, bp_ref, wc_ref, vtc_ref):
    e = pl.program_id(1)
    n_tok = n2t_ref.shape[1]
    n_exp = u_ref.shape[0]
    piece = DENSE_PIECE
    n_piece = n_exp // piece
    first_keys = piece // PEER_KEYS
    n_i = n_exp // PEER_KEYS
    n_tc = n_tok // LANES
    assert n_tc % 2 == 0
    first, second = range(0, n_tc // 2), range(n_tc // 2, n_tc)
    cols_a = slice(0, n_tok // 2)
    cols_b = slice(n_tok // 2, n_tok)

    @pl.when(e == 0)
    def _():
        acc_ref[...] = jnp.zeros_like(acc_ref)
        wc_ref[...] = jnp.zeros_like(wc_ref)
        vtc_ref[...] = jnp.zeros_like(vtc_ref)

    for h in range(PEER_HEADS):
        for ii in range(n_i):
            brho_ref[h * n_i + ii] = jnp.broadcast_to(rho_ref[h, ii:ii + 1, :], (SUBLANES, n_tok))
            bp_ref[h * n_i + ii] = jnp.broadcast_to(p_ref[h, ii:ii + 1, :], (SUBLANES, n_tok))

    def pre_matmuls(cols):
        for pc in range(n_piece):
            rows = slice(pc * piece, (pc + 1) * piece)
            pre_ref[rows, cols] = _dot(u_ref[rows, :], n2t_ref[:, cols])

    pre_matmuls(cols_a)
    acc_ref[:, cols_b] += _dot(vtc_ref[...], wc_ref[...])
    pre_matmuls(cols_b)

    def gate(pc, tcs):
        i0 = pc * first_keys
        nvr = GATE_ROWS // SUBLANES
        for tc in tcs:
            cs = slice(tc * LANES, (tc + 1) * LANES)
            for jt in range(PEER_KEYS // GATE_ROWS):
                js = slice(jt * GATE_ROWS, (jt + 1) * GATE_ROWS)
                g = [None] * first_keys
                for h in range(PEER_HEADS):
                    r = r_ref[h, js, cs].reshape(nvr, SUBLANES, LANES)
                    for k in range(first_keys):
                        rho = brho_ref[h * n_i + i0 + k, :, cs][None]
                        p = bp_ref[h * n_i + i0 + k, :, cs][None]
                        term = jnp.where(r >= rho, p * r, 0.0)
                        g[k] = term if g[k] is None else g[k] + term
                for k in range(first_keys):
                    row0 = (i0 + k) * PEER_KEYS + jt * GATE_ROWS
                    pre = pre_ref[row0:row0 + GATE_ROWS, cs]
                    act = pre * (1.0 + lax.erf(pre * math.sqrt(0.5)))
                    w_ref[row0:row0 + GATE_ROWS, cs] = (g[k].reshape(GATE_ROWS, LANES) * act).astype(BF16)

    for pc in range(n_piece):
        gate(pc, first)
    acc_ref[:, cols_a] += _dot(vt_ref[...], w_ref[:, cols_a])
    for pc in range(n_piece):
        gate(pc, second)
    wc_ref[...] = w_ref[:, cols_b]
    vtc_ref[...] = vt_ref[...]

    @pl.when(e == pl.num_programs(1) - 1)
    def _():
        acc_ref[:, cols_b] += _dot(vt_ref[...], wc_ref[...])
        o_ref[...] = h1_ref[...] + acc_ref[...].T


def _dense(n2t, u, vt, rho, p, r, h1, tokens, experts):
    d, t = n2t.shape
    ne = u.shape[0]
    tokens = min(tokens, t)
    grid = (t // tokens, ne // experts)
    n_i = experts // PEER_KEYS
    return pl.pallas_call(
        _dense_kernel, grid=grid,
        in_specs=[pl.BlockSpec((d, tokens), lambda ti, ei: (0, ti)),
                  pl.BlockSpec((experts, d), lambda ti, ei: (ei, 0)),
                  pl.BlockSpec((d, experts), lambda ti, ei: (0, ei)),
                  pl.BlockSpec((PEER_HEADS, n_i, tokens), lambda ti, ei: (0, ei, ti)),
                  pl.BlockSpec((PEER_HEADS, n_i, tokens), lambda ti, ei: (0, ei, ti)),
                  pl.BlockSpec((PEER_HEADS, PEER_KEYS, tokens), lambda ti, ei: (0, 0, ti)),
                  pl.BlockSpec((tokens, d), lambda ti, ei: (ti, 0))],
        out_specs=pl.BlockSpec((tokens, d), lambda ti, ei: (ti, 0)),
        out_shape=jax.ShapeDtypeStruct((t, d), F32),
        scratch_shapes=[pltpu.VMEM((d, tokens), F32),
                        pltpu.VMEM((experts, tokens), BF16),
                        pltpu.VMEM((experts, tokens), F32),
                        pltpu.VMEM((PEER_HEADS * n_i, SUBLANES, tokens), F32),
                        pltpu.VMEM((PEER_HEADS * n_i, SUBLANES, tokens), F32),
                        pltpu.VMEM((experts, tokens // 2), BF16),
                        pltpu.VMEM((d, experts), BF16)],
        compiler_params=pltpu.CompilerParams(dimension_semantics=("arbitrary", "arbitrary"),
                                             vmem_limit_bytes=VMEM_LIMIT),
        name="dense",
    )(n2t, u, vt, rho, p, r, h1)


def _rope_tables(pos):
    half = MLA_ROPE // 2
    inv_freq = ROPE_THETA ** (-jnp.arange(half, dtype=F32) / half)
    ang = pos.astype(F32)[:, None] * inv_freq[None, :]
    cos, sin = jnp.cos(ang), jnp.sin(ang)
    n = pos.shape[0]
    one = jnp.ones((n, MLA_NOPE), F32)
    zero = jnp.zeros((n, MLA_NOPE), F32)
    tail1 = jnp.ones((n, LANES - MLA_QK), F32)
    tail0 = jnp.zeros((n, LANES - MLA_QK), F32)
    zh = jnp.zeros((n, half), F32)
    cos_t = jnp.concatenate([one, cos, cos, tail1], axis=1)
    sina = jnp.concatenate([zero, -sin, zh, tail0], axis=1)
    sinb = jnp.concatenate([zero, zh, sin, tail0], axis=1)
    return cos_t, sina, sinb


def _bucket_np(dist):
    n = np.maximum(dist, 0)
    max_exact = NUM_BUCKETS // 2
    nf = np.maximum(n, 1).astype(np.float32)
    large = max_exact + (np.log(nf / np.float32(max_exact)) / np.float32(math.log(MAX_DISTANCE / max_exact))
                         * np.float32(NUM_BUCKETS - max_exact)).astype(np.int32)
    large = np.minimum(large, NUM_BUCKETS - 1)
    return np.where(n < max_exact, n, large)


def _bias_tiles(rel_bias, tq):
    far = int(np.min(np.nonzero(_bucket_np(np.arange(4 * MAX_DISTANCE)) == NUM_BUCKETS - 1)[0]))
    assert np.all(_bucket_np(np.arange(far, 8 * MAX_DISTANCE)) == NUM_BUCKETS - 1)
    if tq < far:
        raise NotImplementedError("attention tile smaller than the relative-bias window")
    table = rel_bias.astype(F32)
    shifted = (table - table[NUM_BUCKETS - 1][None, :]) * LOG2E

    def toeplitz(n_rows, n_cols, off):
        period = n_rows + n_cols - 1
        dist = np.arange(period) - (n_cols - 1) + off
        vals = jnp.where((dist >= 0)[:, None], shifted[_bucket_np(np.maximum(dist, 0))], NEG)
        u = vals[::-1].T
        flat = jnp.tile(u, (1, n_rows + 1))[:, :n_rows * (period + 1)]
        return flat.reshape(-1, n_rows, period + 1)[:, ::-1, :n_cols]

    diag = toeplitz(tq, tq, 0)
    prev = toeplitz(tq, tq, tq)
    pad = jnp.asarray(_mask_tiles(tq)[0][:, 0])
    nh = rel_bias.shape[1]
    meta0 = jnp.where(pad[0][None] < 0.0, NEG, toeplitz(tq, LANES, N_META))
    meta_far = jnp.broadcast_to(pad[1][None], (nh, tq, LANES))
    return jnp.stack([meta0, meta_far]), diag, prev


def _mask_tiles(tq):
    r = np.arange(tq)[:, None]
    c = np.arange(tq)[None, :]
    diag = np.where(r >= c, 0.0, NEG).astype(np.float32)[None]
    cm = np.arange(LANES)[None, :]
    meta = np.broadcast_to(np.where(cm < N_META, 0.0, NEG).astype(np.float32), (tq, LANES))
    return np.stack([meta, meta])[:, None], diag


def _pad_lanes(a, width):
    return jnp.pad(a, [(0, 0)] * (a.ndim - 1) + [(0, width - a.shape[-1])])


def kernel(x, meta_tokens, rel_bias, attn_norm, w_in, mla_q_norm, mla_w_uq, mla_kv_norm, mla_w_ukv,
           mla_qk_norm_q, mla_qk_norm_k, diff_q_norm, diff_k_norm, diff_lambda_q1, diff_lambda_k1,
           diff_lambda_q2, diff_lambda_k2, diff_subln, w_out, ffn_norm, peer_w_query, peer_sub_keys,
           peer_u, peer_v):
    b, s, d = x.shape
    assert attn_norm.shape[0] == 1, "meta rows are only used as keys: single layer"
    lambda_init = 0.8 - 0.6 * math.exp(-0.3 * 0)
    tq = min(ATTN_TILE, s)
    assert s % tq == 0

    wi = w_in[0]
    s1, s2, s3 = MLA_Q_RANK, MLA_Q_RANK + MLA_KV_RANK, MLA_Q_RANK + MLA_KV_RANK + MLA_ROPE
    wkr = jnp.pad(wi[:, s2:s3], ((0, 0), (MLA_NOPE, LANES - MLA_QK)))
    w_in_arr = jnp.concatenate([wi[:, :s2], wkr, wi[:, s3:]], axis=1).astype(BF16)
    wuq = _pad_lanes(mla_w_uq[0].reshape(MLA_Q_RANK, MLA_HEADS, MLA_QK), LANES).reshape(MLA_Q_RANK, -1).astype(BF16)
    wukv = mla_w_ukv[0].reshape(MLA_KV_RANK, MLA_HEADS, MLA_NOPE + MLA_V)
    wk = _pad_lanes(wukv[..., :MLA_NOPE], LANES).reshape(MLA_KV_RANK, -1)
    wv = _pad_lanes(wukv[..., MLA_NOPE:], LANES).reshape(MLA_KV_RANK, -1)
    wkv = jnp.concatenate([wk, wv], axis=1).astype(BF16)
    row = lambda a: a.reshape(1, -1).astype(F32)
    gq = _pad_lanes(row(mla_qk_norm_q[0]), LANES)
    gk = _pad_lanes(row(mla_qk_norm_k[0]), LANES)
    gdq = jnp.tile(row(diff_q_norm[0]), (1, 2))
    gdk = jnp.tile(row(diff_k_norm[0]), (1, 2))
    wts = (row(attn_norm[0]), w_in_arr, row(mla_q_norm[0]), wuq, row(mla_kv_norm[0]), wkv, gq, gk, gdq, gdk)

    pos_real = jnp.arange(N_META, N_META + s, dtype=jnp.int32)
    qm, km, vm, qd, kd, vd = _project(x, _rope_tables(pos_real), wts, PROJ_ROWS)
    pos_meta = jnp.arange(N_META, dtype=jnp.int32)
    _, km_m, vm_m, _, kd_m, vd_m = _project(meta_tokens[None].astype(x.dtype), _rope_tables(pos_meta), wts, N_META)
    padk = lambda a: jnp.pad(a[0], ((0, 0), (0, LANES - N_META), (0, 0)))

    mmeta, mdiag = (jnp.asarray(a) for a in _mask_tiles(tq))
    y_mla = _flash(qm[:, :, None], km, vm, padk(km_m), padk(vm_m), mmeta, mdiag, None, (),
                   diff=False, tq=tq, lambda_init=lambda_init)
    bmeta, bdiag, bprev = _bias_tiles(rel_bias, tq)
    extra = (row(diff_lambda_q1[0]), row(diff_lambda_k1[0]), row(diff_lambda_q2[0]), row(diff_lambda_k2[0]),
             row(diff_subln[0]))
    y_diff = _flash(qd, kd, vd, padk(kd_m), padk(vd_m), bmeta, bdiag, bprev, extra,
                    diff=True, tq=tq, lambda_init=lambda_init)

    t = b * s
    wo = w_out[0].astype(BF16)
    n_mla = MLA_HEADS * MLA_V
    wqt = peer_w_query[0].T.astype(BF16)
    sk = peer_sub_keys[0].reshape(PEER_HEADS * 2, PEER_KEYS, PEER_SUB).astype(BF16)
    h1, n2t, g_rho, g_p, g_r = _front(x.reshape(t, d), y_mla.reshape(t, -1), y_diff.reshape(t, -1),
                                      wo[:n_mla], wo[n_mla:], row(ffn_norm[0]), wqt, sk, FRONT_ROWS)
    u_bf = peer_u[0].astype(BF16)
    vt_bf = peer_v[0].T.astype(BF16)
    out = _dense(n2t, u_bf, vt_bf, g_rho, g_p, g_r, h1, DENSE_TOKENS, DENSE_EXPERTS)
    return out.reshape(b, s, d)
```

```python
import functools
import math

import numpy as np
import jax
import jax.numpy as jnp
from jax import lax
from jax.experimental import pallas as pl
from jax.experimental.pallas import tpu as pltpu

F32 = jnp.float32
BF16 = jnp.bfloat16

N_META = 16
EPS = 1e-6
LANES = 128
SUBLANES = 8
PACKED_ROWS = 16

MLA_HEADS = 8
MLA_Q_RANK = 256
MLA_KV_RANK = 256
MLA_NOPE = 64
MLA_ROPE = 32
MLA_V = 64
MLA_QK = MLA_NOPE + MLA_ROPE
ROPE_THETA = 10000.0

DIFF_HEADS = 4
DIFF_HD = 64
DIFF_V = 2 * DIFF_HD

NUM_BUCKETS = 32
MAX_DISTANCE = 128

PEER_HEADS = 8
PEER_KEYS = 128
PEER_TOPK = 16
PEER_SUB = 128

NEG = -1e30

PROJ_ROWS = 512
PROJ_SUB_ROWS = 256
ATTN_TILE = 512
ATTN_CHAIN_ROWS = 256
LOG2E = math.log2(math.e)
FRONT_ROWS = 256
DENSE_TOKENS = 512
DENSE_EXPERTS = 1024
DENSE_PIECE = 256
GATE_ROWS = 64
VMEM_LIMIT = 56 * 1024 * 1024


def _rms(x, g):
    return x * lax.rsqrt(jnp.mean(x * x, axis=-1, keepdims=True) + EPS) * g


def _dot(a, b):
    return jnp.dot(a, b, preferred_element_type=F32)


def _lane_tile(a, n):
    return a if n == 1 else jnp.concatenate([a] * n, axis=1)


def _dot_nt(a, b):
    return lax.dot_general(a, b, (((1,), (1,)), ((), ())), preferred_element_type=F32)


def _proj_kernel(x_ref, g1_ref, win_ref, gcq_ref, wuq_ref, gckv_ref, wkv_ref, gq_ref, gk_ref,
                 gdq_ref, gdk_ref, cos_ref, sina_ref, sinb_ref,
                 qm_ref, km_ref, vm_ref, qd_ref, kd_ref, vd_ref, *, scale_m, scale_d):
    rows = x_ref.shape[1]
    sub = min(PROJ_SUB_ROWS, rows)
    subs = [slice(i * sub, (i + 1) * sub) for i in range(rows // sub)]
    ns = [_rms(x_ref[0, sl, :], g1_ref[...]).astype(BF16) for sl in subs]
    projs = [_dot(n, win_ref[...]) for n in ns]
    cqs = [_rms(p[:, 0:256], gcq_ref[...]).astype(BF16) for p in projs]
    ckvs = [_rms(p[:, 256:512], gckv_ref[...]).astype(BF16) for p in projs]
    q_raws = [_dot(c, wuq_ref[...]) for c in cqs]
    kvs = [_dot(c, wkv_ref[...]) for c in ckvs]
    gq = gq_ref[...]
    gk = gk_ref[...]
    gdq = gdq_ref[...]
    gdk = gdk_ref[...]
    lo_v = lax.broadcasted_iota(jnp.int32, (1, LANES), 1) < MLA_V
    lo = lax.broadcasted_iota(jnp.int32, (1, LANES), 1) < DIFF_HD
    inv_qk = 1.0 / MLA_QK
    inv_hd = 1.0 / DIFF_HD

    def halfnorm(t, g):
        sq = t * t
        s_lo = jnp.sum(jnp.where(lo, sq, 0.0), axis=-1, keepdims=True)
        s_hi = jnp.sum(jnp.where(lo, 0.0, sq), axis=-1, keepdims=True)
        r = jnp.where(lo, lax.rsqrt(s_lo * inv_hd + EPS), lax.rsqrt(s_hi * inv_hd + EPS))
        return t * r * g

    for sl, proj, q_raw, kv in zip(subs, projs, q_raws, kvs):
        krp = proj[:, 512:640]
        dq = proj[:, 640:1152]
        dk = proj[:, 1152:1664]
        dv = proj[:, 1664:2176]
        cos = cos_ref[sl, :]
        sina = sina_ref[sl, :]
        sinb = sinb_ref[sl, :]

        def rope(t, cos=cos, sina=sina, sinb=sinb):
            return t * cos + pltpu.roll(t, LANES - 16, 1) * sina + pltpu.roll(t, 16, 1) * sinb

        for h in range(MLA_HEADS):
            hl = slice(h * LANES, (h + 1) * LANES)
            qh = q_raw[:, hl]
            rq = lax.rsqrt(jnp.sum(qh * qh, axis=-1, keepdims=True) * inv_qk + EPS)
            qm_ref[0, h, sl, :] = (rope(qh * rq * gq) * scale_m).astype(BF16)
            kh = kv[:, hl] + krp
            rk = lax.rsqrt(jnp.sum(kh * kh, axis=-1, keepdims=True) * inv_qk + EPS)
            km_ref[0, h, sl, :] = rope(kh * rk * gk).astype(BF16)
            vm_ref[0, h, sl, :] = jnp.where(lo_v, kv[:, 1024 + h * LANES:1024 + (h + 1) * LANES], 1.0).astype(BF16)
        for h in range(DIFF_HEADS):
            hl = slice(h * LANES, (h + 1) * LANES)
            qn = halfnorm(dq[:, hl], gdq) * scale_d
            qd_ref[0, h, 0, sl, :] = jnp.where(lo, qn, 0.0).astype(BF16)
            qd_ref[0, h, 1, sl, :] = jnp.where(lo, 0.0, qn).astype(BF16)
            kd_ref[0, h, sl, :] = halfnorm(dk[:, hl], gdk).astype(BF16)
            vd_ref[0, h, sl, 0:LANES] = dv[:, hl].astype(BF16)
            vd_ref[0, h, sl, LANES:2 * LANES] = jnp.ones_like(dv[:, hl]).astype(BF16)


def _project(x3, tabs, wts, rows):
    b, s, d = x3.shape
    rows = min(rows, s)
    assert s % rows == 0
    grid = (b, s // rows)
    full = lambda a: pl.BlockSpec(a.shape, lambda i, j: (0,) * a.ndim)
    tab_spec = pl.BlockSpec((rows, LANES), lambda i, j: (j, 0))
    in_specs = [pl.BlockSpec((1, rows, d), lambda i, j: (i, j, 0))] + [full(w) for w in wts] + [tab_spec] * 3
    hs = lambda nh: pl.BlockSpec((1, nh, rows, LANES), lambda i, j: (i, 0, j, 0))
    out_shape = (
        jax.ShapeDtypeStruct((b, MLA_HEADS, s, LANES), BF16),
        jax.ShapeDtypeStruct((b, MLA_HEADS, s, LANES), BF16),
        jax.ShapeDtypeStruct((b, MLA_HEADS, s, LANES), BF16),
        jax.ShapeDtypeStruct((b, DIFF_HEADS, 2, s, LANES), BF16),
        jax.ShapeDtypeStruct((b, DIFF_HEADS, s, LANES), BF16),
        jax.ShapeDtypeStruct((b, DIFF_HEADS, s, 2 * LANES), BF16),
    )
    out_specs = (hs(MLA_HEADS), hs(MLA_HEADS), hs(MLA_HEADS),
                 pl.BlockSpec((1, DIFF_HEADS, 2, rows, LANES), lambda i, j: (i, 0, 0, j, 0)),
                 hs(DIFF_HEADS),
                 pl.BlockSpec((1, DIFF_HEADS, rows, 2 * LANES), lambda i, j: (i, 0, j, 0)))
    kern = functools.partial(_proj_kernel, scale_m=MLA_QK ** -0.5 * LOG2E, scale_d=DIFF_HD ** -0.5 * LOG2E)
    return pl.pallas_call(
        kern, grid=grid, in_specs=in_specs, out_specs=out_specs, out_shape=out_shape,
        compiler_params=pltpu.CompilerParams(dimension_semantics=("arbitrary", "arbitrary"),
                                             vmem_limit_bytes=VMEM_LIMIT),
        name="proj",
    )(x3, *wts, *tabs)


def _flash_kernel(*refs, hp, nm, tq, rc, diff, lambda_init):
    if diff:
        (q_ref, k_ref, v_ref, kmeta_ref, vmeta_ref, bmeta_ref, bdiag_ref, bprev_ref,
         lq1_ref, lk1_ref, lq2_ref, lk2_ref, subln_ref, o_ref, m_ref, acc_ref) = refs
    else:
        q_ref, k_ref, v_ref, kmeta_ref, vmeta_ref, bmeta_ref, bdiag_ref, o_ref, m_ref, acc_ref = refs
        bprev_ref = None
    qi = pl.program_id(2)
    nchunk = tq // rc
    nv = acc_ref.shape[1]
    chains = [(hh, mm, c) for hh in range(hp) for mm in range(nm) for c in range(nchunk)]
    qs = [q_ref[0, hh, mm, c * rc:(c + 1) * rc, :] for (hh, mm, c) in chains]

    def update(i, s, vt, first=False):
        rows = slice(i * rc, (i + 1) * rc)
        m_cur = jnp.max(s, axis=-1, keepdims=True)
        if first:
            m_new = jnp.broadcast_to(m_cur, (rc, LANES))
        else:
            m_prev = m_ref[rows]
            m_new = jnp.maximum(m_prev, m_cur)
        p = jnp.exp2((s - _lane_tile(m_new, s.shape[1] // LANES)).astype(BF16))
        pv = _dot(p, vt)
        if first:
            acc_ref[rows] = pv
        else:
            alpha = jnp.exp2(m_prev - m_new)
            acc_ref[rows] = _lane_tile(alpha, nv // LANES) * acc_ref[rows] + pv
        m_ref[rows] = m_new

    for i, (hh, mm, c) in enumerate(chains):
        s = _dot_nt(qs[i], kmeta_ref[hh]) + bmeta_ref[0, 0, c * rc:(c + 1) * rc, :]
        update(i, s, vmeta_ref[hh], first=True)

    def step(j, bias_ref=None):
        off = pl.multiple_of(j * tq, tq)
        scores = []
        for i, (hh, mm, c) in enumerate(chains):
            s = _dot_nt(qs[i], k_ref[0, hh, pl.ds(off, tq), :])
            if bias_ref is not None:
                s = s + bias_ref[0, c * rc:(c + 1) * rc, :]
            scores.append(s)
        for i, (hh, mm, c) in enumerate(chains):
            update(i, scores[i], v_ref[0, hh, pl.ds(off, tq), :])

    def plain(j, carry):
        step(j)
        return carry

    if diff:
        lax.fori_loop(0, jnp.maximum(qi - 1, 0), plain, 0)
        pl.when(qi > 0)(lambda: step(qi - 1, bprev_ref))
    else:
        lax.fori_loop(0, qi, plain, 0)
    step(qi, bdiag_ref)

    def normalized(i):
        acc = acc_ref[i * rc:(i + 1) * rc]
        return acc[:, :LANES] / acc[:, nv - 1:nv]

    outs = {ch: normalized(i) for i, ch in enumerate(chains)}
    if diff:
        lam = (jnp.exp(jnp.sum(lq1_ref[...] * lk1_ref[...], axis=-1, keepdims=True))
               - jnp.exp(jnp.sum(lq2_ref[...] * lk2_ref[...], axis=-1, keepdims=True)) + lambda_init)
        for c in range(nchunk):
            d = outs[(0, 0, c)] - lam * outs[(0, 1, c)]
            y = _rms(d, subln_ref[...]) * (1.0 - lambda_init)
            o_ref[0, c * rc:(c + 1) * rc, :] = y.astype(BF16)
    else:
        lo = lax.broadcasted_iota(jnp.int32, (1, LANES), 1) < MLA_V
        for c in range(nchunk):
            pair = jnp.where(lo, outs[(0, 0, c)], pltpu.roll(outs[(1, 0, c)], MLA_V, 1))
            o_ref[0, c * rc:(c + 1) * rc, :] = pair.astype(BF16)


def _flash(q, k, v, kmeta, vmeta, bmeta, bdiag, bprev, extra, *, diff, tq, lambda_init):
    b, g, nm, s, _ = q.shape
    nv = v.shape[-1]
    hp = 1 if diff else 2
    ngrid = g // hp
    nq = s // tq
    rows = tq
    rc = min(ATTN_CHAIN_ROWS, tq)
    grid = (b, ngrid, nq)
    gsel = (lambda gi: gi) if diff else (lambda gi: 0)
    in_specs = [
        pl.BlockSpec((1, hp, nm, tq, LANES), lambda bi, gi, qi: (bi, gi, 0, qi, 0)),
        pl.BlockSpec((1, hp, s, LANES), lambda bi, gi, qi: (bi, gi, 0, 0)),
        pl.BlockSpec((1, hp, s, nv), lambda bi, gi, qi: (bi, gi, 0, 0)),
        pl.BlockSpec((hp, LANES, LANES), lambda bi, gi, qi: (gi, 0, 0)),
        pl.BlockSpec((hp, LANES, nv), lambda bi, gi, qi: (gi, 0, 0)),
        pl.BlockSpec((1, 1, rows, LANES), lambda bi, gi, qi: (jnp.minimum(qi, 1), gsel(gi), 0, 0)),
        pl.BlockSpec((1, rows, tq), lambda bi, gi, qi: (gsel(gi), 0, 0)),
    ]
    args = [q, k, v, kmeta, vmeta, bmeta, bdiag]
    if diff:
        in_specs.append(pl.BlockSpec((1, rows, tq), lambda bi, gi, qi: (gi, 0, 0)))
        args.append(bprev)
        for e in extra:
            in_specs.append(pl.BlockSpec(e.shape, lambda bi, gi, qi: (0, 0)))
            args.append(e)
    kern = functools.partial(_flash_kernel, hp=hp, nm=nm, tq=tq, rc=rc, diff=diff, lambda_init=lambda_init)
    return pl.pallas_call(
        kern, grid=grid, in_specs=in_specs,
        out_specs=pl.BlockSpec((1, tq, LANES), lambda bi, gi, qi: (bi, qi, gi)),
        out_shape=jax.ShapeDtypeStruct((b, s, ngrid * LANES), BF16),
        scratch_shapes=[pltpu.VMEM((hp * nm * tq, LANES), F32), pltpu.VMEM((hp * nm * tq, nv), F32)],
        compiler_params=pltpu.CompilerParams(dimension_semantics=("arbitrary",) * 3,
                                             vmem_limit_bytes=VMEM_LIMIT),
        name="flash_diff" if diff else "flash_mla",
    )(*args)


def _top_values(s, n, row_ref=None):
    vals = []
    for i in range(n):
        m = jnp.max(s, axis=0, keepdims=True)
        vals.append(m)
        if row_ref is not None and i < PEER_TOPK:
            row_ref[i:i + 1, :] = m
        if i + 1 < n:
            s = jnp.where(s == m, -jnp.inf, s)
    return vals


def _front_kernel(x_ref, ym_ref, yd_ref, woa_ref, wob_ref, g2_ref, wqt_ref, sk_ref,
                  h1_ref, n2t_ref, rho_ref, p_ref, r_ref, v1s_ref, v2s_ref, cand_ref):
    h1 = x_ref[...] + _dot(ym_ref[...], woa_ref[...]) + _dot(yd_ref[...], wob_ref[...])
    h1_ref[...] = h1
    n2t = _rms(h1, g2_ref[...]).T.astype(BF16)
    n2t_ref[...] = n2t
    qpt = _dot(wqt_ref[...], n2t).astype(BF16)
    k = PEER_TOPK
    for h in range(PEER_HEADS):
        s1 = _dot(sk_ref[2 * h], qpt[(2 * h) * PEER_SUB:(2 * h + 1) * PEER_SUB])
        s2 = _dot(sk_ref[2 * h + 1], qpt[(2 * h + 1) * PEER_SUB:(2 * h + 2) * PEER_SUB])
        v1 = _top_values(s1, k + 1, v1s_ref)
        v2 = _top_values(s2, k + 1, v2s_ref)
        v2lo = v2s_ref[0:SUBLANES, :]
        row = lax.broadcasted_iota(jnp.int32, v2lo.shape, 0)
        cand_ref[0:k, :] = v1[0] + v2s_ref[...]
        cand_ref[k:k + SUBLANES, :] = v1[1] + v2lo
        for a in range(2, SUBLANES):
            lo = k + (a - 1) * SUBLANES
            cand_ref[lo:lo + SUBLANES, :] = jnp.where(row < (k + 1) // (a + 1), v1[a] + v2lo, -jnp.inf)
        cand_ref[k + 7 * SUBLANES:k + 8 * SUBLANES, :] = v1s_ref[SUBLANES:k, :] + v2[0]
        c = _top_values(cand_ref[...], k + 1)
        c17 = jnp.maximum(c[k], jnp.maximum(v1[k] + v2[0], v1[0] + v2[k]))
        tau = 0.5 * (c[k - 1] + c17)
        z = jnp.exp(c[0] - c[0])
        for i in range(1, k):
            z = z + jnp.exp(c[i] - c[0])
        rho = jnp.full(s1.shape, jnp.inf, F32)
        for l in range(k):
            rho = jnp.where(s1 + v2[l] >= tau, jnp.exp(v2[l] - v2[0]), rho)
        p = jnp.exp(s1 - v1[0]) * (0.5 / z)
        r = jnp.exp(s2 - v2[0])
        for c in range(s1.shape[1] // LANES):
            cs = slice(c * LANES, (c + 1) * LANES)
            rho_ref[h, c] = rho[:, cs]
            p_ref[h, c] = p[:, cs]
            r_ref[h, c] = r[:, cs]


def _front(x2, ym, yd, woa, wob, g2, wqt, sk, rows):
    t, d = x2.shape
    rows = min(rows, t)
    grid = (t // rows,)
    full = lambda a: pl.BlockSpec(a.shape, lambda i: (0,) * a.ndim)
    assert rows % LANES == 0
    gate = jax.ShapeDtypeStruct((PEER_HEADS, t // LANES, PEER_KEYS, LANES), F32)
    gate_spec = pl.BlockSpec((PEER_HEADS, rows // LANES, PEER_KEYS, LANES), lambda i: (0, i, 0, 0))
    return pl.pallas_call(
        _front_kernel, grid=grid,
        in_specs=[pl.BlockSpec((rows, d), lambda i: (i, 0)),
                  pl.BlockSpec((rows, ym.shape[1]), lambda i: (i, 0)),
                  pl.BlockSpec((rows, yd.shape[1]), lambda i: (i, 0)),
                  full(woa), full(wob), full(g2), full(wqt), full(sk)],
        out_specs=(pl.BlockSpec((rows, d), lambda i: (i, 0)),
                   pl.BlockSpec((d, rows), lambda i: (0, i)),
                   gate_spec, gate_spec, gate_spec),
        out_shape=(jax.ShapeDtypeStruct((t, d), F32), jax.ShapeDtypeStruct((d, t), BF16),
                   gate, gate, gate),
        scratch_shapes=[pltpu.VMEM((PEER_TOPK, rows), F32), pltpu.VMEM((PEER_TOPK, rows), F32),
                        pltpu.VMEM((PEER_TOPK + 8 * SUBLANES, rows), F32)],
        compiler_params=pltpu.CompilerParams(dimension_semantics=("arbitrary",),
                                             vmem_limit_bytes=VMEM_LIMIT),
        name="front",
    )(x2, ym, yd, woa, wob, g2, wqt, sk)


def _dense_kernel(n2t_ref, u_ref, vt_ref, rho_ref, p_ref, r_ref, h1_ref, o_ref,
                  acc_ref, w_ref, pre_ref, brho_ref, bp_ref, wc_ref, vtc_ref):
    e = pl.program_id(1)
    n_tok = n2t_ref.shape[1]
    n_exp = u_ref.shape[0]
    piece = DENSE_PIECE
    n_piece = n_exp // piece
    first_keys = piece // PEER_KEYS
    n_i = n_exp // PEER_KEYS
    n_tc = n_tok // LANES
    assert n_tc % 2 == 0
    n_half = n_tc // 2
    first, second = range(0, n_half), range(n_half, n_tc)

    def lanes_of(ref, tcs, rows=slice(None)):
        return jnp.concatenate([ref[tc, rows, :] for tc in tcs], axis=1)

    @pl.when(e == 0)
    def _():
        acc_ref[...] = jnp.zeros_like(acc_ref)
        wc_ref[...] = jnp.zeros_like(wc_ref)
        vtc_ref[...] = jnp.zeros_like(vtc_ref)

    for h in range(PEER_HEADS):
        for ii in range(n_i):
            for tc in range(n_tc):
                brho_ref[h * n_i + ii, tc] = jnp.broadcast_to(rho_ref[h, tc, ii:ii + 1, :], (SUBLANES, LANES))
                bp_ref[h * n_i + ii, tc] = jnp.broadcast_to(p_ref[h, tc, ii:ii + 1, :], (SUBLANES, LANES))

    def pre_matmuls(tcs):
        cols = slice(tcs[0] * LANES, (tcs[-1] + 1) * LANES)
        for pc in range(n_piece):
            rows = slice(pc * piece, (pc + 1) * piece)
            val = _dot(u_ref[rows, :], n2t_ref[:, cols])
            for j, tc in enumerate(tcs):
                pre_ref[tc, rows, :] = val[:, j * LANES:(j + 1) * LANES]

    pre_matmuls(first)
    acc_ref[1] += _dot(vtc_ref[...], lanes_of(wc_ref, range(n_half)))
    pre_matmuls(second)

    def gate(pc, tcs):
        i0 = pc * first_keys
        nvr = GATE_ROWS // SUBLANES
        for tc in tcs:
            for jt in range(PEER_KEYS // GATE_ROWS):
                js = slice(jt * GATE_ROWS, (jt + 1) * GATE_ROWS)
                g = [None] * first_keys
                for h in range(PEER_HEADS):
                    r = r_ref[h, tc, js, :].reshape(nvr, SUBLANES, LANES)
                    for k in range(first_keys):
                        rho = brho_ref[h * n_i + i0 + k, tc][None]
                        p = bp_ref[h * n_i + i0 + k, tc][None]
                        term = jnp.where(r >= rho, p * r, 0.0)
                        g[k] = term if g[k] is None else g[k] + term
                for k in range(first_keys):
                    row0 = (i0 + k) * PEER_KEYS + jt * GATE_ROWS
                    pre = pre_ref[tc, row0:row0 + GATE_ROWS, :]
                    act = pre * (1.0 + lax.erf(pre * math.sqrt(0.5)))
                    w_ref[tc, row0:row0 + GATE_ROWS, :] = (g[k].reshape(GATE_ROWS, LANES) * act).astype(BF16)

    for pc in range(n_piece):
        gate(pc, first)
    acc_ref[0] += _dot(vt_ref[...], lanes_of(w_ref, first))
    for pc in range(n_piece):
        gate(pc, second)
    for j, tc in enumerate(second):
        wc_ref[j] = w_ref[tc]
    vtc_ref[...] = vt_ref[...]

    @pl.when(e == pl.num_programs(1) - 1)
    def _():
        acc_b = acc_ref[1] + _dot(vt_ref[...], lanes_of(wc_ref, range(n_half)))
        acc = jnp.concatenate([acc_ref[0], acc_b], axis=1)
        o_ref[...] = h1_ref[...] + acc.T


def _dense(n2t, u, vt, rho, p, r, h1, tokens, experts):
    d, t = n2t.shape
    ne = u.shape[0]
    tokens = min(tokens, t)
    grid = (t // tokens, ne // experts)
    n_i = experts // PEER_KEYS
    n_tc = tokens // LANES
    return pl.pallas_call(
        _dense_kernel, grid=grid,
        in_specs=[pl.BlockSpec((d, tokens), lambda ti, ei: (0, ti)),
                  pl.BlockSpec((experts, d), lambda ti, ei: (ei, 0)),
                  pl.BlockSpec((d, experts), lambda ti, ei: (0, ei)),
                  pl.BlockSpec((PEER_HEADS, n_tc, n_i, LANES), lambda ti, ei: (0, ti, ei, 0)),
                  pl.BlockSpec((PEER_HEADS, n_tc, n_i, LANES), lambda ti, ei: (0, ti, ei, 0)),
                  pl.BlockSpec((PEER_HEADS, n_tc, PEER_KEYS, LANES), lambda ti, ei: (0, ti, 0, 0)),
                  pl.BlockSpec((tokens, d), lambda ti, ei: (ti, 0))],
        out_specs=pl.BlockSpec((tokens, d), lambda ti, ei: (ti, 0)),
        out_shape=jax.ShapeDtypeStruct((t, d), F32),
        scratch_shapes=[pltpu.VMEM((2, d, tokens // 2), F32),
                        pltpu.VMEM((n_tc, experts, LANES), BF16),
                        pltpu.VMEM((n_tc, experts, LANES), F32),
                        pltpu.VMEM((PEER_HEADS * n_i, n_tc, SUBLANES, LANES), F32),
                        pltpu.VMEM((PEER_HEADS * n_i, n_tc, SUBLANES, LANES), F32),
                        pltpu.VMEM((n_tc // 2, experts, LANES), BF16),
                        pltpu.VMEM((d, experts), BF16)],
        compiler_params=pltpu.CompilerParams(dimension_semantics=("arbitrary", "arbitrary"),
                                             vmem_limit_bytes=VMEM_LIMIT),
        name="dense",
    )(n2t, u, vt, rho, p, r, h1)


def _rope_tables(pos):
    half = MLA_ROPE // 2
    inv_freq = ROPE_THETA ** (-jnp.arange(half, dtype=F32) / half)
    ang = pos.astype(F32)[:, None] * inv_freq[None, :]
    cos, sin = jnp.cos(ang), jnp.sin(ang)
    n = pos.shape[0]
    one = jnp.ones((n, MLA_NOPE), F32)
    zero = jnp.zeros((n, MLA_NOPE), F32)
    tail1 = jnp.ones((n, LANES - MLA_QK), F32)
    tail0 = jnp.zeros((n, LANES - MLA_QK), F32)
    zh = jnp.zeros((n, half), F32)
    cos_t = jnp.concatenate([one, cos, cos, tail1], axis=1)
    sina = jnp.concatenate([zero, -sin, zh, tail0], axis=1)
    sinb = jnp.concatenate([zero, zh, sin, tail0], axis=1)
    return cos_t, sina, sinb


def _bucket_np(dist):
    n = np.maximum(dist, 0)
    max_exact = NUM_BUCKETS // 2
    nf = np.maximum(n, 1).astype(np.float32)
    large = max_exact + (np.log(nf / np.float32(max_exact)) / np.float32(math.log(MAX_DISTANCE / max_exact))
                         * np.float32(NUM_BUCKETS - max_exact)).astype(np.int32)
    large = np.minimum(large, NUM_BUCKETS - 1)
    return np.where(n < max_exact, n, large)


def _bias_tiles(rel_bias, tq):
    far = int(np.min(np.nonzero(_bucket_np(np.arange(4 * MAX_DISTANCE)) == NUM_BUCKETS - 1)[0]))
    assert np.all(_bucket_np(np.arange(far, 8 * MAX_DISTANCE)) == NUM_BUCKETS - 1)
    if tq < far:
        raise NotImplementedError("attention tile smaller than the relative-bias window")
    table = rel_bias.astype(F32)
    shifted = (table - table[NUM_BUCKETS - 1][None, :]) * LOG2E

    def toeplitz(n_rows, n_cols, off):
        period = n_rows + n_cols - 1
        dist = np.arange(period) - (n_cols - 1) + off
        vals = jnp.where((dist >= 0)[:, None], shifted[_bucket_np(np.maximum(dist, 0))], NEG)
        u = vals[::-1].T
        flat = jnp.tile(u, (1, n_rows + 1))[:, :n_rows * (period + 1)]
        return flat.reshape(-1, n_rows, period + 1)[:, ::-1, :n_cols]

    diag = toeplitz(tq, tq, 0)
    prev = toeplitz(tq, tq, tq)
    pad = jnp.asarray(_mask_tiles(tq)[0][:, 0])
    nh = rel_bias.shape[1]
    meta0 = jnp.where(pad[0][None] < 0.0, NEG, toeplitz(tq, LANES, N_META))
    meta_far = jnp.broadcast_to(pad[1][None], (nh, tq, LANES))
    return jnp.stack([meta0, meta_far]), diag, prev


def _mask_tiles(tq):
    r = np.arange(tq)[:, None]
    c = np.arange(tq)[None, :]
    diag = np.where(r >= c, 0.0, NEG).astype(np.float32)[None]
    cm = np.arange(LANES)[None, :]
    meta = np.broadcast_to(np.where(cm < N_META, 0.0, NEG).astype(np.float32), (tq, LANES))
    return np.stack([meta, meta])[:, None], diag


def _pad_lanes(a, width):
    return jnp.pad(a, [(0, 0)] * (a.ndim - 1) + [(0, width - a.shape[-1])])


def kernel(x, meta_tokens, rel_bias, attn_norm, w_in, mla_q_norm, mla_w_uq, mla_kv_norm, mla_w_ukv,
           mla_qk_norm_q, mla_qk_norm_k, diff_q_norm, diff_k_norm, diff_lambda_q1, diff_lambda_k1,
           diff_lambda_q2, diff_lambda_k2, diff_subln, w_out, ffn_norm, peer_w_query, peer_sub_keys,
           peer_u, peer_v):
    b, s, d = x.shape
    assert attn_norm.shape[0] == 1, "meta rows are only used as keys: single layer"
    lambda_init = 0.8 - 0.6 * math.exp(-0.3 * 0)
    tq = min(ATTN_TILE, s)
    assert s % tq == 0

    wi = w_in[0]
    s1, s2, s3 = MLA_Q_RANK, MLA_Q_RANK + MLA_KV_RANK, MLA_Q_RANK + MLA_KV_RANK + MLA_ROPE
    wkr = jnp.pad(wi[:, s2:s3], ((0, 0), (MLA_NOPE, LANES - MLA_QK)))
    w_in_arr = jnp.concatenate([wi[:, :s2], wkr, wi[:, s3:]], axis=1).astype(BF16)
    wuq = _pad_lanes(mla_w_uq[0].reshape(MLA_Q_RANK, MLA_HEADS, MLA_QK), LANES).reshape(MLA_Q_RANK, -1).astype(BF16)
    wukv = mla_w_ukv[0].reshape(MLA_KV_RANK, MLA_HEADS, MLA_NOPE + MLA_V)
    wk = _pad_lanes(wukv[..., :MLA_NOPE], LANES).reshape(MLA_KV_RANK, -1)
    wv = _pad_lanes(wukv[..., MLA_NOPE:], LANES).reshape(MLA_KV_RANK, -1)
    wkv = jnp.concatenate([wk, wv], axis=1).astype(BF16)
    row = lambda a: a.reshape(1, -1).astype(F32)
    gq = _pad_lanes(row(mla_qk_norm_q[0]), LANES)
    gk = _pad_lanes(row(mla_qk_norm_k[0]), LANES)
    gdq = jnp.tile(row(diff_q_norm[0]), (1, 2))
    gdk = jnp.tile(row(diff_k_norm[0]), (1, 2))
    wts = (row(attn_norm[0]), w_in_arr, row(mla_q_norm[0]), wuq, row(mla_kv_norm[0]), wkv, gq, gk, gdq, gdk)

    pos_real = jnp.arange(N_META, N_META + s, dtype=jnp.int32)
    qm, km, vm, qd, kd, vd = _project(x, _rope_tables(pos_real), wts, PROJ_ROWS)
    pos_meta = jnp.arange(N_META, dtype=jnp.int32)
    _, km_m, vm_m, _, kd_m, vd_m = _project(meta_tokens[None].astype(x.dtype), _rope_tables(pos_meta), wts, N_META)
    padk = lambda a: jnp.pad(a[0], ((0, 0), (0, LANES - N_META), (0, 0)))

    mmeta, mdiag = (jnp.asarray(a) for a in _mask_tiles(tq))
    y_mla = _flash(qm[:, :, None], km, vm, padk(km_m), padk(vm_m), mmeta, mdiag, None, (),
                   diff=False, tq=tq, lambda_init=lambda_init)
    bmeta, bdiag, bprev = _bias_tiles(rel_bias, tq)
    extra = (row(diff_lambda_q1[0]), row(diff_lambda_k1[0]), row(diff_lambda_q2[0]), row(diff_lambda_k2[0]),
             row(diff_subln[0]))
    y_diff = _flash(qd, kd, vd, padk(kd_m), padk(vd_m), bmeta, bdiag, bprev, extra,
                    diff=True, tq=tq, lambda_init=lambda_init)

    t = b * s
    wo = w_out[0].astype(BF16)
    n_mla = MLA_HEADS * MLA_V
    wqt = peer_w_query[0].T.astype(BF16)
    sk = peer_sub_keys[0].reshape(PEER_HEADS * 2, PEER_KEYS, PEER_SUB).astype(BF16)
    h1, n2t, g_rho, g_p, g_r = _front(x.reshape(t, d), y_mla.reshape(t, -1), y_diff.reshape(t, -1),
                                      wo[:n_mla], wo[n_mla:], row(ffn_norm[0]), wqt, sk, FRONT_ROWS)
    u_bf = peer_u[0].astype(BF16)
    vt_bf = peer_v[0].T.astype(BF16)
    out = _dense(n2t, u_bf, vt_bf, g_rho, g_p, g_r, h1, DENSE_TOKENS, DENSE_EXPERTS)
    return out.reshape(b, s, d)
```

```python
import functools
import math

import numpy as np
import jax
import jax.numpy as jnp
from jax import lax
from jax.experimental import pallas as pl
from jax.experimental.pallas import tpu as pltpu

F32 = jnp.float32
BF16 = jnp.bfloat16

N_META = 16
EPS = 1e-6
LANES = 128
SUBLANES = 8
PACKED_ROWS = 16

MLA_HEADS = 8
MLA_Q_RANK = 256
MLA_KV_RANK = 256
MLA_NOPE = 64
MLA_ROPE = 32
MLA_V = 64
MLA_QK = MLA_NOPE + MLA_ROPE
ROPE_THETA = 10000.0

DIFF_HEADS = 4
DIFF_HD = 64
DIFF_V = 2 * DIFF_HD

NUM_BUCKETS = 32
MAX_DISTANCE = 128

PEER_HEADS = 8
PEER_KEYS = 128
PEER_TOPK = 16
PEER_SUB = 128

NEG = -1e30

PROJ_ROWS = 512
PROJ_SUB_ROWS = 256
ATTN_TILE = 512
ATTN_CHAIN_ROWS = 256
LOG2E = math.log2(math.e)
FRONT_ROWS = 256
DENSE_TOKENS = 512
DENSE_EXPERTS = 2048
DENSE_PIECE = 256
GATE_ROWS = 64
VMEM_LIMIT = 56 * 1024 * 1024


def _rms(x, g):
    return x * lax.rsqrt(jnp.mean(x * x, axis=-1, keepdims=True) + EPS) * g


def _dot(a, b):
    return jnp.dot(a, b, preferred_element_type=F32)


def _lane_tile(a, n):
    return a if n == 1 else jnp.concatenate([a] * n, axis=1)


def _dot_nt(a, b):
    return lax.dot_general(a, b, (((1,), (1,)), ((), ())), preferred_element_type=F32)


def _proj_kernel(x_ref, g1_ref, win_ref, gcq_ref, wuq_ref, gckv_ref, wkv_ref, gq_ref, gk_ref,
                 gdq_ref, gdk_ref, cos_ref, sina_ref, sinb_ref,
                 qm_ref, km_ref, vm_ref, qd_ref, kd_ref, vd_ref, *, scale_m, scale_d):
    rows = x_ref.shape[1]
    sub = min(PROJ_SUB_ROWS, rows)
    subs = [slice(i * sub, (i + 1) * sub) for i in range(rows // sub)]
    ns = [_rms(x_ref[0, sl, :], g1_ref[...]).astype(BF16) for sl in subs]
    projs = [_dot(n, win_ref[...]) for n in ns]
    cqs = [_rms(p[:, 0:256], gcq_ref[...]).astype(BF16) for p in projs]
    ckvs = [_rms(p[:, 256:512], gckv_ref[...]).astype(BF16) for p in projs]
    q_raws = [_dot(c, wuq_ref[...]) for c in cqs]
    kvs = [_dot(c, wkv_ref[...]) for c in ckvs]
    gq = gq_ref[...]
    gk = gk_ref[...]
    gdq = gdq_ref[...]
    gdk = gdk_ref[...]
    lo_v = lax.broadcasted_iota(jnp.int32, (1, LANES), 1) < MLA_V
    lo = lax.broadcasted_iota(jnp.int32, (1, LANES), 1) < DIFF_HD
    inv_qk = 1.0 / MLA_QK
    inv_hd = 1.0 / DIFF_HD

    def halfnorm(t, g):
        sq = t * t
        s_lo = jnp.sum(jnp.where(lo, sq, 0.0), axis=-1, keepdims=True)
        s_hi = jnp.sum(jnp.where(lo, 0.0, sq), axis=-1, keepdims=True)
        r = jnp.where(lo, lax.rsqrt(s_lo * inv_hd + EPS), lax.rsqrt(s_hi * inv_hd + EPS))
        return t * r * g

    for sl, proj, q_raw, kv in zip(subs, projs, q_raws, kvs):
        krp = proj[:, 512:640]
        dq = proj[:, 640:1152]
        dk = proj[:, 1152:1664]
        dv = proj[:, 1664:2176]
        cos = cos_ref[sl, :]
        sina = sina_ref[sl, :]
        sinb = sinb_ref[sl, :]

        def rope(t, cos=cos, sina=sina, sinb=sinb):
            return t * cos + pltpu.roll(t, LANES - 16, 1) * sina + pltpu.roll(t, 16, 1) * sinb

        for h in range(MLA_HEADS):
            hl = slice(h * LANES, (h + 1) * LANES)
            qh = q_raw[:, hl]
            rq = lax.rsqrt(jnp.sum(qh * qh, axis=-1, keepdims=True) * inv_qk + EPS)
            qm_ref[0, h, sl, :] = (rope(qh * rq * gq) * scale_m).astype(BF16)
            kh = kv[:, hl] + krp
            rk = lax.rsqrt(jnp.sum(kh * kh, axis=-1, keepdims=True) * inv_qk + EPS)
            km_ref[0, h, sl, :] = rope(kh * rk * gk).astype(BF16)
            vm_ref[0, h, sl, :] = jnp.where(lo_v, kv[:, 1024 + h * LANES:1024 + (h + 1) * LANES], 1.0).astype(BF16)
        for h in range(DIFF_HEADS):
            hl = slice(h * LANES, (h + 1) * LANES)
            qn = halfnorm(dq[:, hl], gdq) * scale_d
            qd_ref[0, h, 0, sl, :] = jnp.where(lo, qn, 0.0).astype(BF16)
            qd_ref[0, h, 1, sl, :] = jnp.where(lo, 0.0, qn).astype(BF16)
            kd_ref[0, h, sl, :] = halfnorm(dk[:, hl], gdk).astype(BF16)
            vd_ref[0, h, sl, 0:LANES] = dv[:, hl].astype(BF16)
            vd_ref[0, h, sl, LANES:2 * LANES] = jnp.ones_like(dv[:, hl]).astype(BF16)


def _project(x3, tabs, wts, rows):
    b, s, d = x3.shape
    rows = min(rows, s)
    assert s % rows == 0
    grid = (b, s // rows)
    full = lambda a: pl.BlockSpec(a.shape, lambda i, j: (0,) * a.ndim)
    tab_spec = pl.BlockSpec((rows, LANES), lambda i, j: (j, 0))
    in_specs = [pl.BlockSpec((1, rows, d), lambda i, j: (i, j, 0))] + [full(w) for w in wts] + [tab_spec] * 3
    hs = lambda nh: pl.BlockSpec((1, nh, rows, LANES), lambda i, j: (i, 0, j, 0))
    out_shape = (
        jax.ShapeDtypeStruct((b, MLA_HEADS, s, LANES), BF16),
        jax.ShapeDtypeStruct((b, MLA_HEADS, s, LANES), BF16),
        jax.ShapeDtypeStruct((b, MLA_HEADS, s, LANES), BF16),
        jax.ShapeDtypeStruct((b, DIFF_HEADS, 2, s, LANES), BF16),
        jax.ShapeDtypeStruct((b, DIFF_HEADS, s, LANES), BF16),
        jax.ShapeDtypeStruct((b, DIFF_HEADS, s, 2 * LANES), BF16),
    )
    out_specs = (hs(MLA_HEADS), hs(MLA_HEADS), hs(MLA_HEADS),
                 pl.BlockSpec((1, DIFF_HEADS, 2, rows, LANES), lambda i, j: (i, 0, 0, j, 0)),
                 hs(DIFF_HEADS),
                 pl.BlockSpec((1, DIFF_HEADS, rows, 2 * LANES), lambda i, j: (i, 0, j, 0)))
    kern = functools.partial(_proj_kernel, scale_m=MLA_QK ** -0.5 * LOG2E, scale_d=DIFF_HD ** -0.5 * LOG2E)
    return pl.pallas_call(
        kern, grid=grid, in_specs=in_specs, out_specs=out_specs, out_shape=out_shape,
        compiler_params=pltpu.CompilerParams(dimension_semantics=("arbitrary", "arbitrary"),
                                             vmem_limit_bytes=VMEM_LIMIT),
        name="proj",
    )(x3, *wts, *tabs)


def _flash_kernel(*refs, hp, nm, tq, rc, diff, lambda_init):
    if diff:
        (q_ref, k_ref, v_ref, kmeta_ref, vmeta_ref, bmeta_ref, bdiag_ref, bprev_ref,
         lq1_ref, lk1_ref, lq2_ref, lk2_ref, subln_ref, o_ref, m_ref, acc_ref) = refs
    else:
        q_ref, k_ref, v_ref, kmeta_ref, vmeta_ref, bmeta_ref, bdiag_ref, o_ref, m_ref, acc_ref = refs
        bprev_ref = None
    qi = pl.program_id(2)
    nchunk = tq // rc
    nv = acc_ref.shape[1]
    chains = [(hh, mm, c) for hh in range(hp) for mm in range(nm) for c in range(nchunk)]
    qs = [q_ref[0, hh, mm, c * rc:(c + 1) * rc, :] for (hh, mm, c) in chains]

    def update(i, s, vt, first=False):
        rows = slice(i * rc, (i + 1) * rc)
        m_cur = jnp.max(s, axis=-1, keepdims=True)
        if first:
            m_new = jnp.broadcast_to(m_cur, (rc, LANES))
        else:
            m_prev = m_ref[rows]
            m_new = jnp.maximum(m_prev, m_cur)
        p = jnp.exp2((s - _lane_tile(m_new, s.shape[1] // LANES)).astype(BF16))
        pv = _dot(p, vt)
        if first:
            acc_ref[rows] = pv
        else:
            alpha = jnp.exp2(m_prev - m_new)
            acc_ref[rows] = _lane_tile(alpha, nv // LANES) * acc_ref[rows] + pv
        m_ref[rows] = m_new

    meta_scores = [_dot_nt(qs[i], kmeta_ref[hh]) + bmeta_ref[0, 0, c * rc:(c + 1) * rc, :]
                   for i, (hh, mm, c) in enumerate(chains)]
    for i, (hh, mm, c) in enumerate(chains):
        update(i, meta_scores[i], vmeta_ref[hh], first=True)

    def step(j, bias_ref=None):
        off = pl.multiple_of(j * tq, tq)
        scores = []
        for i, (hh, mm, c) in enumerate(chains):
            s = _dot_nt(qs[i], k_ref[0, hh, pl.ds(off, tq), :])
            if bias_ref is not None:
                s = s + bias_ref[0, c * rc:(c + 1) * rc, :]
            scores.append(s)
        for i, (hh, mm, c) in enumerate(chains):
            update(i, scores[i], v_ref[0, hh, pl.ds(off, tq), :])

    def plain(j, carry):
        step(j)
        return carry

    if diff:
        lax.fori_loop(0, jnp.maximum(qi - 1, 0), plain, 0)
        pl.when(qi > 0)(lambda: step(qi - 1, bprev_ref))
    else:
        lax.fori_loop(0, qi, plain, 0)
    step(qi, bdiag_ref)

    def normalized(i):
        acc = acc_ref[i * rc:(i + 1) * rc]
        return acc[:, :LANES] / acc[:, nv - 1:nv]

    outs = {ch: normalized(i) for i, ch in enumerate(chains)}
    if diff:
        lam = (jnp.exp(jnp.sum(lq1_ref[...] * lk1_ref[...], axis=-1, keepdims=True))
               - jnp.exp(jnp.sum(lq2_ref[...] * lk2_ref[...], axis=-1, keepdims=True)) + lambda_init)
        for c in range(nchunk):
            d = outs[(0, 0, c)] - lam * outs[(0, 1, c)]
            y = _rms(d, subln_ref[...]) * (1.0 - lambda_init)
            o_ref[0, c * rc:(c + 1) * rc, :] = y.astype(BF16)
    else:
        lo = lax.broadcasted_iota(jnp.int32, (1, LANES), 1) < MLA_V
        for c in range(nchunk):
            pair = jnp.where(lo, outs[(0, 0, c)], pltpu.roll(outs[(1, 0, c)], MLA_V, 1))
            o_ref[0, c * rc:(c + 1) * rc, :] = pair.astype(BF16)


def _flash(q, k, v, kmeta, vmeta, bmeta, bdiag, bprev, extra, *, diff, tq, lambda_init):
    b, g, nm, s, _ = q.shape
    nv = v.shape[-1]
    hp = 1 if diff else 2
    ngrid = g // hp
    nq = s // tq
    rows = tq
    rc = min(ATTN_CHAIN_ROWS, tq)
    grid = (b, ngrid, nq)
    gsel = (lambda gi: gi) if diff else (lambda gi: 0)
    in_specs = [
        pl.BlockSpec((1, hp, nm, tq, LANES), lambda bi, gi, qi: (bi, gi, 0, qi, 0)),
        pl.BlockSpec((1, hp, s, LANES), lambda bi, gi, qi: (bi, gi, 0, 0)),
        pl.BlockSpec((1, hp, s, nv), lambda bi, gi, qi: (bi, gi, 0, 0)),
        pl.BlockSpec((hp, LANES, LANES), lambda bi, gi, qi: (gi, 0, 0)),
        pl.BlockSpec((hp, LANES, nv), lambda bi, gi, qi: (gi, 0, 0)),
        pl.BlockSpec((1, 1, rows, LANES), lambda bi, gi, qi: (jnp.minimum(qi, 1), gsel(gi), 0, 0)),
        pl.BlockSpec((1, rows, tq), lambda bi, gi, qi: (gsel(gi), 0, 0)),
    ]
    args = [q, k, v, kmeta, vmeta, bmeta, bdiag]
    if diff:
        in_specs.append(pl.BlockSpec((1, rows, tq), lambda bi, gi, qi: (gi, 0, 0)))
        args.append(bprev)
        for e in extra:
            in_specs.append(pl.BlockSpec(e.shape, lambda bi, gi, qi: (0, 0)))
            args.append(e)
    kern = functools.partial(_flash_kernel, hp=hp, nm=nm, tq=tq, rc=rc, diff=diff, lambda_init=lambda_init)
    return pl.pallas_call(
        kern, grid=grid, in_specs=in_specs,
        out_specs=pl.BlockSpec((1, tq, LANES), lambda bi, gi, qi: (bi, qi, gi)),
        out_shape=jax.ShapeDtypeStruct((b, s, ngrid * LANES), BF16),
        scratch_shapes=[pltpu.VMEM((hp * nm * tq, LANES), F32), pltpu.VMEM((hp * nm * tq, nv), F32)],
        compiler_params=pltpu.CompilerParams(dimension_semantics=("arbitrary",) * 3,
                                             vmem_limit_bytes=VMEM_LIMIT),
        name="flash_diff" if diff else "flash_mla",
    )(*args)


def _top_values(s, n, row_ref=None):
    vals = []
    for i in range(n):
        m = jnp.max(s, axis=0, keepdims=True)
        vals.append(m)
        if row_ref is not None and i < PEER_TOPK:
            row_ref[i:i + 1, :] = m
        if i + 1 < n:
            s = jnp.where(s == m, -jnp.inf, s)
    return vals


def _front_kernel(x_ref, ym_ref, yd_ref, woa_ref, wob_ref, g2_ref, wqt_ref, sk_ref,
                  h1_ref, n2t_ref, rho_ref, p_ref, r_ref, v1s_ref, v2s_ref, cand_ref):
    h1 = x_ref[...] + _dot(ym_ref[...], woa_ref[...]) + _dot(yd_ref[...], wob_ref[...])
    h1_ref[...] = h1
    n2t = _rms(h1, g2_ref[...]).T.astype(BF16)
    n2t_ref[...] = n2t
    qpt = _dot(wqt_ref[...], n2t).astype(BF16)
    k = PEER_TOPK
    for h in range(PEER_HEADS):
        s1 = _dot(sk_ref[2 * h], qpt[(2 * h) * PEER_SUB:(2 * h + 1) * PEER_SUB])
        s2 = _dot(sk_ref[2 * h + 1], qpt[(2 * h + 1) * PEER_SUB:(2 * h + 2) * PEER_SUB])
        v1 = _top_values(s1, k + 1, v1s_ref)
        v2 = _top_values(s2, k + 1, v2s_ref)
        v2lo = v2s_ref[0:SUBLANES, :]
        row = lax.broadcasted_iota(jnp.int32, v2lo.shape, 0)
        cand_ref[0:k, :] = v1[0] + v2s_ref[...]
        cand_ref[k:k + SUBLANES, :] = v1[1] + v2lo
        for a in range(2, SUBLANES):
            lo = k + (a - 1) * SUBLANES
            cand_ref[lo:lo + SUBLANES, :] = jnp.where(row < (k + 1) // (a + 1), v1[a] + v2lo, -jnp.inf)
        cand_ref[k + 7 * SUBLANES:k + 8 * SUBLANES, :] = v1s_ref[SUBLANES:k, :] + v2[0]
        c = _top_values(cand_ref[...], k + 1)
        c17 = jnp.maximum(c[k], jnp.maximum(v1[k] + v2[0], v1[0] + v2[k]))
        tau = 0.5 * (c[k - 1] + c17)
        z = jnp.exp(c[0] - c[0])
        for i in range(1, k):
            z = z + jnp.exp(c[i] - c[0])
        rho = jnp.full(s1.shape, jnp.inf, F32)
        for l in range(k):
            rho = jnp.where(s1 + v2[l] >= tau, jnp.exp(v2[l] - v2[0]), rho)
        p = jnp.exp(s1 - v1[0]) * (0.5 / z)
        r = jnp.exp(s2 - v2[0])
        for c in range(s1.shape[1] // LANES):
            cs = slice(c * LANES, (c + 1) * LANES)
            rho_ref[h, c] = rho[:, cs]
            p_ref[h, c] = p[:, cs]
            r_ref[h, c] = r[:, cs]


def _front(x2, ym, yd, woa, wob, g2, wqt, sk, rows):
    t, d = x2.shape
    rows = min(rows, t)
    grid = (t // rows,)
    full = lambda a: pl.BlockSpec(a.shape, lambda i: (0,) * a.ndim)
    assert rows % LANES == 0
    gate = jax.ShapeDtypeStruct((PEER_HEADS, t // LANES, PEER_KEYS, LANES), F32)
    gate_spec = pl.BlockSpec((PEER_HEADS, rows // LANES, PEER_KEYS, LANES), lambda i: (0, i, 0, 0))
    return pl.pallas_call(
        _front_kernel, grid=grid,
        in_specs=[pl.BlockSpec((rows, d), lambda i: (i, 0)),
                  pl.BlockSpec((rows, ym.shape[1]), lambda i: (i, 0)),
                  pl.BlockSpec((rows, yd.shape[1]), lambda i: (i, 0)),
                  full(woa), full(wob), full(g2), full(wqt), full(sk)],
        out_specs=(pl.BlockSpec((rows, d), lambda i: (i, 0)),
                   pl.BlockSpec((d, rows), lambda i: (0, i)),
                   gate_spec, gate_spec, gate_spec),
        out_shape=(jax.ShapeDtypeStruct((t, d), F32), jax.ShapeDtypeStruct((d, t), BF16),
                   gate, gate, gate),
        scratch_shapes=[pltpu.VMEM((PEER_TOPK, rows), F32), pltpu.VMEM((PEER_TOPK, rows), F32),
                        pltpu.VMEM((PEER_TOPK + 8 * SUBLANES, rows), F32)],
        compiler_params=pltpu.CompilerParams(dimension_semantics=("arbitrary",),
                                             vmem_limit_bytes=VMEM_LIMIT),
        name="front",
    )(x2, ym, yd, woa, wob, g2, wqt, sk)


def _dense_kernel(n2t_ref, u_ref, vt_ref, rho_ref, p_ref, r_ref, h1_ref, o_ref,
                  acc_ref, w_ref, pre_ref, brho_ref, bp_ref, wc_ref, vtc_ref):
    e = pl.program_id(1)
    n_tok = n2t_ref.shape[1]
    n_exp = u_ref.shape[0]
    piece = DENSE_PIECE
    n_piece = n_exp // piece
    first_keys = piece // PEER_KEYS
    n_i = n_exp // PEER_KEYS
    n_tc = n_tok // LANES
    assert n_tc % 2 == 0
    n_half = n_tc // 2
    first, second = range(0, n_half), range(n_half, n_tc)

    def lanes_of(ref, tcs, rows=slice(None)):
        return jnp.concatenate([ref[tc, rows, :] for tc in tcs], axis=1)

    @pl.when(e == 0)
    def _():
        acc_ref[...] = jnp.zeros_like(acc_ref)
        wc_ref[...] = jnp.zeros_like(wc_ref)
        vtc_ref[...] = jnp.zeros_like(vtc_ref)

    for h in range(PEER_HEADS):
        for ii in range(n_i):
            for tc in range(n_tc):
                brho_ref[h * n_i + ii, tc] = jnp.broadcast_to(rho_ref[h, tc, ii:ii + 1, :], (SUBLANES, LANES))
                bp_ref[h * n_i + ii, tc] = jnp.broadcast_to(p_ref[h, tc, ii:ii + 1, :], (SUBLANES, LANES))

    def pre_matmuls(tcs):
        cols = slice(tcs[0] * LANES, (tcs[-1] + 1) * LANES)
        for pc in range(n_piece):
            rows = slice(pc * piece, (pc + 1) * piece)
            val = _dot(u_ref[rows, :], n2t_ref[:, cols])
            for j, tc in enumerate(tcs):
                pre_ref[tc, rows, :] = val[:, j * LANES:(j + 1) * LANES]

    pre_matmuls(first)
    acc_ref[1] += _dot(vtc_ref[...], lanes_of(wc_ref, range(n_half)))
    pre_matmuls(second)

    def gate(pc, tcs):
        i0 = pc * first_keys
        nvr = GATE_ROWS // SUBLANES
        for tc in tcs:
            for jt in range(PEER_KEYS // GATE_ROWS):
                js = slice(jt * GATE_ROWS, (jt + 1) * GATE_ROWS)
                g = [None] * first_keys
                for h in range(PEER_HEADS):
                    r = r_ref[h, tc, js, :].reshape(nvr, SUBLANES, LANES)
                    for k in range(first_keys):
                        rho = brho_ref[h * n_i + i0 + k, tc][None]
                        p = bp_ref[h * n_i + i0 + k, tc][None]
                        term = jnp.where(r >= rho, p * r, 0.0)
                        g[k] = term if g[k] is None else g[k] + term
                for k in range(first_keys):
                    row0 = (i0 + k) * PEER_KEYS + jt * GATE_ROWS
                    pre = pre_ref[tc, row0:row0 + GATE_ROWS, :]
                    act = pre * (1.0 + lax.erf(pre * math.sqrt(0.5)))
                    w_ref[tc, row0:row0 + GATE_ROWS, :] = (g[k].reshape(GATE_ROWS, LANES) * act).astype(BF16)

    for pc in range(n_piece):
        gate(pc, first)
    acc_ref[0] += _dot(vt_ref[...], lanes_of(w_ref, first))
    for pc in range(n_piece):
        gate(pc, second)
    for j, tc in enumerate(second):
        wc_ref[j] = w_ref[tc]
    vtc_ref[...] = vt_ref[...]

    @pl.when(e == pl.num_programs(1) - 1)
    def _():
        acc_b = acc_ref[1] + _dot(vt_ref[...], lanes_of(wc_ref, range(n_half)))
        acc = jnp.concatenate([acc_ref[0], acc_b], axis=1)
        o_ref[...] = h1_ref[...] + acc.T


def _dense(n2t, u, vt, rho, p, r, h1, tokens, experts):
    d, t = n2t.shape
    ne = u.shape[0]
    tokens = min(tokens, t)
    grid = (t // tokens, ne // experts)
    n_i = experts // PEER_KEYS
    n_tc = tokens // LANES
    return pl.pallas_call(
        _dense_kernel, grid=grid,
        in_specs=[pl.BlockSpec((d, tokens), lambda ti, ei: (0, ti)),
                  pl.BlockSpec((experts, d), lambda ti, ei: (ei, 0)),
                  pl.BlockSpec((d, experts), lambda ti, ei: (0, ei)),
                  pl.BlockSpec((PEER_HEADS, n_tc, n_i, LANES), lambda ti, ei: (0, ti, ei, 0)),
                  pl.BlockSpec((PEER_HEADS, n_tc, n_i, LANES), lambda ti, ei: (0, ti, ei, 0)),
                  pl.BlockSpec((PEER_HEADS, n_tc, PEER_KEYS, LANES), lambda ti, ei: (0, ti, 0, 0)),
                  pl.BlockSpec((tokens, d), lambda ti, ei: (ti, 0))],
        out_specs=pl.BlockSpec((tokens, d), lambda ti, ei: (ti, 0)),
        out_shape=jax.ShapeDtypeStruct((t, d), F32),
        scratch_shapes=[pltpu.VMEM((2, d, tokens // 2), F32),
                        pltpu.VMEM((n_tc, experts, LANES), BF16),
                        pltpu.VMEM((n_tc, experts, LANES), F32),
                        pltpu.VMEM((PEER_HEADS * n_i, n_tc, SUBLANES, LANES), F32),
                        pltpu.VMEM((PEER_HEADS * n_i, n_tc, SUBLANES, LANES), F32),
                        pltpu.VMEM((n_tc // 2, experts, LANES), BF16),
                        pltpu.VMEM((d, experts), BF16)],
        compiler_params=pltpu.CompilerParams(dimension_semantics=("arbitrary", "arbitrary"),
                                             vmem_limit_bytes=VMEM_LIMIT),
        name="dense",
    )(n2t, u, vt, rho, p, r, h1)


def _rope_tables(pos):
    half = MLA_ROPE // 2
    inv_freq = ROPE_THETA ** (-jnp.arange(half, dtype=F32) / half)
    ang = pos.astype(F32)[:, None] * inv_freq[None, :]
    cos, sin = jnp.cos(ang), jnp.sin(ang)
    n = pos.shape[0]
    one = jnp.ones((n, MLA_NOPE), F32)
    zero = jnp.zeros((n, MLA_NOPE), F32)
    tail1 = jnp.ones((n, LANES - MLA_QK), F32)
    tail0 = jnp.zeros((n, LANES - MLA_QK), F32)
    zh = jnp.zeros((n, half), F32)
    cos_t = jnp.concatenate([one, cos, cos, tail1], axis=1)
    sina = jnp.concatenate([zero, -sin, zh, tail0], axis=1)
    sinb = jnp.concatenate([zero, zh, sin, tail0], axis=1)
    return cos_t, sina, sinb


def _bucket_np(dist):
    n = np.maximum(dist, 0)
    max_exact = NUM_BUCKETS // 2
    nf = np.maximum(n, 1).astype(np.float32)
    large = max_exact + (np.log(nf / np.float32(max_exact)) / np.float32(math.log(MAX_DISTANCE / max_exact))
                         * np.float32(NUM_BUCKETS - max_exact)).astype(np.int32)
    large = np.minimum(large, NUM_BUCKETS - 1)
    return np.where(n < max_exact, n, large)


def _bias_tiles(rel_bias, tq):
    far = int(np.min(np.nonzero(_bucket_np(np.arange(4 * MAX_DISTANCE)) == NUM_BUCKETS - 1)[0]))
    assert np.all(_bucket_np(np.arange(far, 8 * MAX_DISTANCE)) == NUM_BUCKETS - 1)
    if tq < far:
        raise NotImplementedError("attention tile smaller than the relative-bias window")
    table = rel_bias.astype(F32)
    shifted = (table - table[NUM_BUCKETS - 1][None, :]) * LOG2E

    def toeplitz(n_rows, n_cols, off):
        period = n_rows + n_cols - 1
        dist = np.arange(period) - (n_cols - 1) + off
        vals = jnp.where((dist >= 0)[:, None], shifted[_bucket_np(np.maximum(dist, 0))], NEG)
        u = vals[::-1].T
        flat = jnp.tile(u, (1, n_rows + 1))[:, :n_rows * (period + 1)]
        return flat.reshape(-1, n_rows, period + 1)[:, ::-1, :n_cols]

    diag = toeplitz(tq, tq, 0)
    prev = toeplitz(tq, tq, tq)
    pad = jnp.asarray(_mask_tiles(tq)[0][:, 0])
    nh = rel_bias.shape[1]
    meta0 = jnp.where(pad[0][None] < 0.0, NEG, toeplitz(tq, LANES, N_META))
    meta_far = jnp.broadcast_to(pad[1][None], (nh, tq, LANES))
    return jnp.stack([meta0, meta_far]), diag, prev


def _mask_tiles(tq):
    r = np.arange(tq)[:, None]
    c = np.arange(tq)[None, :]
    diag = np.where(r >= c, 0.0, NEG).astype(np.float32)[None]
    cm = np.arange(LANES)[None, :]
    meta = np.broadcast_to(np.where(cm < N_META, 0.0, NEG).astype(np.float32), (tq, LANES))
    return np.stack([meta, meta])[:, None], diag


def _pad_lanes(a, width):
    return jnp.pad(a, [(0, 0)] * (a.ndim - 1) + [(0, width - a.shape[-1])])


def kernel(x, meta_tokens, rel_bias, attn_norm, w_in, mla_q_norm, mla_w_uq, mla_kv_norm, mla_w_ukv,
           mla_qk_norm_q, mla_qk_norm_k, diff_q_norm, diff_k_norm, diff_lambda_q1, diff_lambda_k1,
           diff_lambda_q2, diff_lambda_k2, diff_subln, w_out, ffn_norm, peer_w_query, peer_sub_keys,
           peer_u, peer_v):
    b, s, d = x.shape
    assert attn_norm.shape[0] == 1, "meta rows are only used as keys: single layer"
    lambda_init = 0.8 - 0.6 * math.exp(-0.3 * 0)
    tq = min(ATTN_TILE, s)
    assert s % tq == 0

    wi = w_in[0]
    s1, s2, s3 = MLA_Q_RANK, MLA_Q_RANK + MLA_KV_RANK, MLA_Q_RANK + MLA_KV_RANK + MLA_ROPE
    wkr = jnp.pad(wi[:, s2:s3], ((0, 0), (MLA_NOPE, LANES - MLA_QK)))
    w_in_arr = jnp.concatenate([wi[:, :s2], wkr, wi[:, s3:]], axis=1).astype(BF16)
    wuq = _pad_lanes(mla_w_uq[0].reshape(MLA_Q_RANK, MLA_HEADS, MLA_QK), LANES).reshape(MLA_Q_RANK, -1).astype(BF16)
    wukv = mla_w_ukv[0].reshape(MLA_KV_RANK, MLA_HEADS, MLA_NOPE + MLA_V)
    wk = _pad_lanes(wukv[..., :MLA_NOPE], LANES).reshape(MLA_KV_RANK, -1)
    wv = _pad_lanes(wukv[..., MLA_NOPE:], LANES).reshape(MLA_KV_RANK, -1)
    wkv = jnp.concatenate([wk, wv], axis=1).astype(BF16)
    row = lambda a: a.reshape(1, -1).astype(F32)
    gq = _pad_lanes(row(mla_qk_norm_q[0]), LANES)
    gk = _pad_lanes(row(mla_qk_norm_k[0]), LANES)
    gdq = jnp.tile(row(diff_q_norm[0]), (1, 2))
    gdk = jnp.tile(row(diff_k_norm[0]), (1, 2))
    wts = (row(attn_norm[0]), w_in_arr, row(mla_q_norm[0]), wuq, row(mla_kv_norm[0]), wkv, gq, gk, gdq, gdk)

    pos_real = jnp.arange(N_META, N_META + s, dtype=jnp.int32)
    qm, km, vm, qd, kd, vd = _project(x, _rope_tables(pos_real), wts, PROJ_ROWS)
    pos_meta = jnp.arange(N_META, dtype=jnp.int32)
    _, km_m, vm_m, _, kd_m, vd_m = _project(meta_tokens[None].astype(x.dtype), _rope_tables(pos_meta), wts, N_META)
    padk = lambda a: jnp.pad(a[0], ((0, 0), (0, LANES - N_META), (0, 0)))

    mmeta, mdiag = (jnp.asarray(a) for a in _mask_tiles(tq))
    y_mla = _flash(qm[:, :, None], km, vm, padk(km_m), padk(vm_m), mmeta, mdiag, None, (),
                   diff=False, tq=tq, lambda_init=lambda_init)
    bmeta, bdiag, bprev = _bias_tiles(rel_bias, tq)
    extra = (row(diff_lambda_q1[0]), row(diff_lambda_k1[0]), row(diff_lambda_q2[0]), row(diff_lambda_k2[0]),
             row(diff_subln[0]))
    y_diff = _flash(qd, kd, vd, padk(kd_m), padk(vd_m), bmeta, bdiag, bprev, extra,
                    diff=True, tq=tq, lambda_init=lambda_init)

    t = b * s
    wo = w_out[0].astype(BF16)
    n_mla = MLA_HEADS * MLA_V
    wqt = peer_w_query[0].T.astype(BF16)
    sk = peer_sub_keys[0].reshape(PEER_HEADS * 2, PEER_KEYS, PEER_SUB).astype(BF16)
    h1, n2t, g_rho, g_p, g_r = _front(x.reshape(t, d), y_mla.reshape(t, -1), y_diff.reshape(t, -1),
                                      wo[:n_mla], wo[n_mla:], row(ffn_norm[0]), wqt, sk, FRONT_ROWS)
    u_bf = peer_u[0].astype(BF16)
    vt_bf = peer_v[0].T.astype(BF16)
    out = _dense(n2t, u_bf, vt_bf, g_rho, g_p, g_r, h1, DENSE_TOKENS, DENSE_EXPERTS)
    return out.reshape(b, s, d)
```

```python
import functools
import math

import numpy as np
import jax
import jax.numpy as jnp
from jax import lax
from jax.experimental import pallas as pl
from jax.experimental.pallas import tpu as pltpu

F32 = jnp.float32
BF16 = jnp.bfloat16

N_META = 16
EPS = 1e-6
LANES = 128
SUBLANES = 8
PACKED_ROWS = 16

MLA_HEADS = 8
MLA_Q_RANK = 256
MLA_KV_RANK = 256
MLA_NOPE = 64
MLA_ROPE = 32
MLA_V = 64
MLA_QK = MLA_NOPE + MLA_ROPE
ROPE_THETA = 10000.0

DIFF_HEADS = 4
DIFF_HD = 64
DIFF_V = 2 * DIFF_HD

NUM_BUCKETS = 32
MAX_DISTANCE = 128

PEER_HEADS = 8
PEER_KEYS = 128
PEER_TOPK = 16
PEER_SUB = 128

NEG = -1e30

PROJ_ROWS = 512
PROJ_SUB_ROWS = 256
ATTN_TILE = 512
ATTN_CHAIN_ROWS = 256
LOG2E = math.log2(math.e)
FRONT_ROWS = 256
DENSE_TOKENS = 512
DENSE_EXPERTS = 2048
DENSE_PIECE = 256
GATE_ROWS = 64
VMEM_LIMIT = 56 * 1024 * 1024


def _rms(x, g):
    return x * lax.rsqrt(jnp.mean(x * x, axis=-1, keepdims=True) + EPS) * g


def _dot(a, b):
    return jnp.dot(a, b, preferred_element_type=F32)


def _lane_tile(a, n):
    return a if n == 1 else jnp.concatenate([a] * n, axis=1)


def _dot_nt(a, b):
    return lax.dot_general(a, b, (((1,), (1,)), ((), ())), preferred_element_type=F32)


def _proj_kernel(x_ref, g1_ref, win_ref, gcq_ref, wuq_ref, gckv_ref, wkv_ref, gq_ref, gk_ref,
                 gdq_ref, gdk_ref, cos_ref, sina_ref, sinb_ref,
                 qm_ref, km_ref, vm_ref, qd_ref, kd_ref, vd_ref, *, scale_m, scale_d):
    rows = x_ref.shape[1]
    sub = min(PROJ_SUB_ROWS, rows)
    subs = [slice(i * sub, (i + 1) * sub) for i in range(rows // sub)]
    ns = [_rms(x_ref[0, sl, :], g1_ref[...]).astype(BF16) for sl in subs]
    projs = [_dot(n, win_ref[...]) for n in ns]
    cqs = [_rms(p[:, 0:256], gcq_ref[...]).astype(BF16) for p in projs]
    ckvs = [_rms(p[:, 256:512], gckv_ref[...]).astype(BF16) for p in projs]
    q_raws = [_dot(c, wuq_ref[...]) for c in cqs]
    kvs = [_dot(c, wkv_ref[...]) for c in ckvs]
    gq = gq_ref[...]
    gk = gk_ref[...]
    gdq = gdq_ref[...]
    gdk = gdk_ref[...]
    lo_v = lax.broadcasted_iota(jnp.int32, (1, LANES), 1) < MLA_V
    lo = lax.broadcasted_iota(jnp.int32, (1, LANES), 1) < DIFF_HD
    inv_qk = 1.0 / MLA_QK
    inv_hd = 1.0 / DIFF_HD

    def halfnorm(t, g):
        sq = t * t
        s_lo = jnp.sum(jnp.where(lo, sq, 0.0), axis=-1, keepdims=True)
        s_hi = jnp.sum(jnp.where(lo, 0.0, sq), axis=-1, keepdims=True)
        r = jnp.where(lo, lax.rsqrt(s_lo * inv_hd + EPS), lax.rsqrt(s_hi * inv_hd + EPS))
        return t * r * g

    for sl, proj, q_raw, kv in zip(subs, projs, q_raws, kvs):
        krp = proj[:, 512:640]
        dq = proj[:, 640:1152]
        dk = proj[:, 1152:1664]
        dv = proj[:, 1664:2176]
        cos = cos_ref[sl, :]
        sina = sina_ref[sl, :]
        sinb = sinb_ref[sl, :]

        def rope(t, cos=cos, sina=sina, sinb=sinb):
            return t * cos + pltpu.roll(t, LANES - 16, 1) * sina + pltpu.roll(t, 16, 1) * sinb

        for h in range(MLA_HEADS):
            hl = slice(h * LANES, (h + 1) * LANES)
            qh = q_raw[:, hl]
            rq = lax.rsqrt(jnp.sum(qh * qh, axis=-1, keepdims=True) * inv_qk + EPS)
            qm_ref[0, h, sl, :] = (rope(qh * rq * gq) * scale_m).astype(BF16)
            kh = kv[:, hl] + krp
            rk = lax.rsqrt(jnp.sum(kh * kh, axis=-1, keepdims=True) * inv_qk + EPS)
            km_ref[0, h, sl, :] = rope(kh * rk * gk).astype(BF16)
            vm_ref[0, h, sl, :] = jnp.where(lo_v, kv[:, 1024 + h * LANES:1024 + (h + 1) * LANES], 1.0).astype(BF16)
        for h in range(DIFF_HEADS):
            hl = slice(h * LANES, (h + 1) * LANES)
            qn = halfnorm(dq[:, hl], gdq) * scale_d
            qd_ref[0, h, 0, sl, :] = jnp.where(lo, qn, 0.0).astype(BF16)
            qd_ref[0, h, 1, sl, :] = jnp.where(lo, 0.0, qn).astype(BF16)
            kd_ref[0, h, sl, :] = halfnorm(dk[:, hl], gdk).astype(BF16)
            vd_ref[0, h, sl, 0:LANES] = dv[:, hl].astype(BF16)
            vd_ref[0, h, sl, LANES:2 * LANES] = jnp.ones_like(dv[:, hl]).astype(BF16)


def _project(x3, tabs, wts, rows):
    b, s, d = x3.shape
    rows = min(rows, s)
    assert s % rows == 0
    grid = (b, s // rows)
    full = lambda a: pl.BlockSpec(a.shape, lambda i, j: (0,) * a.ndim)
    tab_spec = pl.BlockSpec((rows, LANES), lambda i, j: (j, 0))
    in_specs = [pl.BlockSpec((1, rows, d), lambda i, j: (i, j, 0))] + [full(w) for w in wts] + [tab_spec] * 3
    hs = lambda nh: pl.BlockSpec((1, nh, rows, LANES), lambda i, j: (i, 0, j, 0))
    out_shape = (
        jax.ShapeDtypeStruct((b, MLA_HEADS, s, LANES), BF16),
        jax.ShapeDtypeStruct((b, MLA_HEADS, s, LANES), BF16),
        jax.ShapeDtypeStruct((b, MLA_HEADS, s, LANES), BF16),
        jax.ShapeDtypeStruct((b, DIFF_HEADS, 2, s, LANES), BF16),
        jax.ShapeDtypeStruct((b, DIFF_HEADS, s, LANES), BF16),
        jax.ShapeDtypeStruct((b, DIFF_HEADS, s, 2 * LANES), BF16),
    )
    out_specs = (hs(MLA_HEADS), hs(MLA_HEADS), hs(MLA_HEADS),
                 pl.BlockSpec((1, DIFF_HEADS, 2, rows, LANES), lambda i, j: (i, 0, 0, j, 0)),
                 hs(DIFF_HEADS),
                 pl.BlockSpec((1, DIFF_HEADS, rows, 2 * LANES), lambda i, j: (i, 0, j, 0)))
    kern = functools.partial(_proj_kernel, scale_m=MLA_QK ** -0.5 * LOG2E, scale_d=DIFF_HD ** -0.5 * LOG2E)
    return pl.pallas_call(
        kern, grid=grid, in_specs=in_specs, out_specs=out_specs, out_shape=out_shape,
        compiler_params=pltpu.CompilerParams(dimension_semantics=("arbitrary", "arbitrary"),
                                             vmem_limit_bytes=VMEM_LIMIT),
        name="proj",
    )(x3, *wts, *tabs)


def _flash_kernel(*refs, hp, nm, tq, rc, diff, lambda_init):
    if diff:
        (q_ref, k_ref, v_ref, kmeta_ref, vmeta_ref, bmeta_ref, bdiag_ref, bprev_ref,
         lq1_ref, lk1_ref, lq2_ref, lk2_ref, subln_ref, o_ref, m_ref, acc_ref) = refs
    else:
        q_ref, k_ref, v_ref, kmeta_ref, vmeta_ref, bmeta_ref, bdiag_ref, o_ref, m_ref, acc_ref = refs
        bprev_ref = None
    qi = pl.program_id(2)
    nchunk = tq // rc
    nv = acc_ref.shape[1]
    chains = [(hh, mm, c) for hh in range(hp) for mm in range(nm) for c in range(nchunk)]
    qs = [q_ref[0, hh, mm, c * rc:(c + 1) * rc, :] for (hh, mm, c) in chains]

    def update(i, s, vt, first=False):
        rows = slice(i * rc, (i + 1) * rc)
        m_cur = jnp.max(s, axis=-1, keepdims=True)
        if first:
            m_new = jnp.broadcast_to(m_cur, (rc, LANES))
        else:
            m_prev = m_ref[rows]
            m_new = jnp.maximum(m_prev, m_cur)
        p = jnp.exp2((s - _lane_tile(m_new, s.shape[1] // LANES)).astype(BF16))
        pv = _dot(p, vt)
        if first:
            acc_ref[rows] = pv
        else:
            alpha = jnp.exp2(m_prev - m_new)
            acc_ref[rows] = _lane_tile(alpha, nv // LANES) * acc_ref[rows] + pv
        m_ref[rows] = m_new

    meta_scores = [_dot_nt(qs[i], kmeta_ref[hh]) + bmeta_ref[0, 0, c * rc:(c + 1) * rc, :]
                   for i, (hh, mm, c) in enumerate(chains)]
    for i, (hh, mm, c) in enumerate(chains):
        update(i, meta_scores[i], vmeta_ref[hh], first=True)

    def step(j, bias_ref=None):
        off = pl.multiple_of(j * tq, tq)
        scores = []
        for i, (hh, mm, c) in enumerate(chains):
            s = _dot_nt(qs[i], k_ref[0, hh, pl.ds(off, tq), :])
            if bias_ref is not None:
                s = s + bias_ref[0, c * rc:(c + 1) * rc, :]
            scores.append(s)
        for i, (hh, mm, c) in enumerate(chains):
            update(i, scores[i], v_ref[0, hh, pl.ds(off, tq), :])

    def plain(j, carry):
        step(j)
        return carry

    if diff:
        lax.fori_loop(0, jnp.maximum(qi - 1, 0), plain, 0)
        pl.when(qi > 0)(lambda: step(qi - 1, bprev_ref))
    else:
        lax.fori_loop(0, qi, plain, 0)
    step(qi, bdiag_ref)

    def normalized(i):
        acc = acc_ref[i * rc:(i + 1) * rc]
        return acc[:, :LANES] / acc[:, nv - 1:nv]

    outs = {ch: normalized(i) for i, ch in enumerate(chains)}
    if diff:
        lam = (jnp.exp(jnp.sum(lq1_ref[...] * lk1_ref[...], axis=-1, keepdims=True))
               - jnp.exp(jnp.sum(lq2_ref[...] * lk2_ref[...], axis=-1, keepdims=True)) + lambda_init)
        for c in range(nchunk):
            d = outs[(0, 0, c)] - lam * outs[(0, 1, c)]
            y = _rms(d, subln_ref[...]) * (1.0 - lambda_init)
            o_ref[0, c * rc:(c + 1) * rc, :] = y.astype(BF16)
    else:
        lo = lax.broadcasted_iota(jnp.int32, (1, LANES), 1) < MLA_V
        for c in range(nchunk):
            pair = jnp.where(lo, outs[(0, 0, c)], pltpu.roll(outs[(1, 0, c)], MLA_V, 1))
            o_ref[0, c * rc:(c + 1) * rc, :] = pair.astype(BF16)


def _flash(q, k, v, kmeta, vmeta, bmeta, bdiag, bprev, extra, *, diff, tq, lambda_init):
    b, g, nm, s, _ = q.shape
    nv = v.shape[-1]
    hp = 1 if diff else 2
    ngrid = g // hp
    nq = s // tq
    rows = tq
    rc = min(ATTN_CHAIN_ROWS, tq)
    grid = (b, ngrid, nq)
    gsel = (lambda gi: gi) if diff else (lambda gi: 0)
    in_specs = [
        pl.BlockSpec((1, hp, nm, tq, LANES), lambda bi, gi, qi: (bi, gi, 0, qi, 0)),
        pl.BlockSpec((1, hp, s, LANES), lambda bi, gi, qi: (bi, gi, 0, 0)),
        pl.BlockSpec((1, hp, s, nv), lambda bi, gi, qi: (bi, gi, 0, 0)),
        pl.BlockSpec((hp, LANES, LANES), lambda bi, gi, qi: (gi, 0, 0)),
        pl.BlockSpec((hp, LANES, nv), lambda bi, gi, qi: (gi, 0, 0)),
        pl.BlockSpec((1, 1, rows, LANES), lambda bi, gi, qi: (jnp.minimum(qi, 1), gsel(gi), 0, 0)),
        pl.BlockSpec((1, rows, tq), lambda bi, gi, qi: (gsel(gi), 0, 0)),
    ]
    args = [q, k, v, kmeta, vmeta, bmeta, bdiag]
    if diff:
        in_specs.append(pl.BlockSpec((1, rows, tq), lambda bi, gi, qi: (gi, 0, 0)))
        args.append(bprev)
        for e in extra:
            in_specs.append(pl.BlockSpec(e.shape, lambda bi, gi, qi: (0, 0)))
            args.append(e)
    kern = functools.partial(_flash_kernel, hp=hp, nm=nm, tq=tq, rc=rc, diff=diff, lambda_init=lambda_init)
    return pl.pallas_call(
        kern, grid=grid, in_specs=in_specs,
        out_specs=pl.BlockSpec((1, tq, LANES), lambda bi, gi, qi: (bi, qi, gi)),
        out_shape=jax.ShapeDtypeStruct((b, s, ngrid * LANES), BF16),
        scratch_shapes=[pltpu.VMEM((hp * nm * tq, LANES), F32), pltpu.VMEM((hp * nm * tq, nv), F32)],
        compiler_params=pltpu.CompilerParams(dimension_semantics=("arbitrary",) * 3,
                                             vmem_limit_bytes=VMEM_LIMIT),
        name="flash_diff" if diff else "flash_mla",
    )(*args)


def _top_values(s, n, row_ref=None):
    vals = []
    for i in range(n):
        m = jnp.max(s, axis=0, keepdims=True)
        vals.append(m)
        if row_ref is not None and i < PEER_TOPK:
            row_ref[i:i + 1, :] = m
        if i + 1 < n:
            s = jnp.where(s == m, -jnp.inf, s)
    return vals


def _front_kernel(x_ref, ym_ref, yd_ref, woa_ref, wob_ref, g2_ref, wqt_ref, sk_ref,
                  h1_ref, n2t_ref, rho_ref, p_ref, r_ref, v1s_ref, v2s_ref, cand_ref):
    h1 = x_ref[...] + _dot(ym_ref[...], woa_ref[...]) + _dot(yd_ref[...], wob_ref[...])
    h1_ref[...] = h1
    n2t = _rms(h1, g2_ref[...]).T.astype(BF16)
    n2t_ref[...] = n2t
    qpt = _dot(wqt_ref[...], n2t).astype(BF16)
    k = PEER_TOPK
    for h in range(PEER_HEADS):
        s1 = _dot(sk_ref[2 * h], qpt[(2 * h) * PEER_SUB:(2 * h + 1) * PEER_SUB])
        s2 = _dot(sk_ref[2 * h + 1], qpt[(2 * h + 1) * PEER_SUB:(2 * h + 2) * PEER_SUB])
        v1 = _top_values(s1, k + 1, v1s_ref)
        v2 = _top_values(s2, k + 1, v2s_ref)
        v2lo = v2s_ref[0:SUBLANES, :]
        row = lax.broadcasted_iota(jnp.int32, v2lo.shape, 0)
        cand_ref[0:k, :] = v1[0] + v2s_ref[...]
        cand_ref[k:k + SUBLANES, :] = v1[1] + v2lo
        for a in range(2, SUBLANES):
            lo = k + (a - 1) * SUBLANES
            cand_ref[lo:lo + SUBLANES, :] = jnp.where(row < (k + 1) // (a + 1), v1[a] + v2lo, -jnp.inf)
        cand_ref[k + 7 * SUBLANES:k + 8 * SUBLANES, :] = v1s_ref[SUBLANES:k, :] + v2[0]
        c = _top_values(cand_ref[...], k + 1)
        c17 = jnp.maximum(c[k], jnp.maximum(v1[k] + v2[0], v1[0] + v2[k]))
        tau = 0.5 * (c[k - 1] + c17)
        z = jnp.exp(c[0] - c[0])
        for i in range(1, k):
            z = z + jnp.exp(c[i] - c[0])
        e2 = [jnp.exp(v2[l] - v2[0]) for l in range(k)]
        n_all = 4
        rho = jnp.full(s1.shape, jnp.inf, F32)
        for l in range(n_all):
            rho = jnp.where(s1 + v2[l] >= tau, e2[l], rho)
        for a in range(k // (n_all + 1)):
            rho_a = jnp.full(tau.shape, jnp.inf, F32)
            for l in range(k // (a + 1)):
                rho_a = jnp.where(v1[a] + v2[l] >= tau, e2[l], rho_a)
            rho = jnp.where(s1 == v1[a], rho_a, rho)
        p = jnp.exp(s1 - v1[0]) * (0.5 / z)
        r = jnp.exp(s2 - v2[0])
        for c in range(s1.shape[1] // LANES):
            cs = slice(c * LANES, (c + 1) * LANES)
            rho_ref[h, c] = rho[:, cs]
            p_ref[h, c] = p[:, cs]
            r_ref[h, c] = r[:, cs]


def _front(x2, ym, yd, woa, wob, g2, wqt, sk, rows):
    t, d = x2.shape
    rows = min(rows, t)
    grid = (t // rows,)
    full = lambda a: pl.BlockSpec(a.shape, lambda i: (0,) * a.ndim)
    assert rows % LANES == 0
    gate = jax.ShapeDtypeStruct((PEER_HEADS, t // LANES, PEER_KEYS, LANES), F32)
    gate_spec = pl.BlockSpec((PEER_HEADS, rows // LANES, PEER_KEYS, LANES), lambda i: (0, i, 0, 0))
    return pl.pallas_call(
        _front_kernel, grid=grid,
        in_specs=[pl.BlockSpec((rows, d), lambda i: (i, 0)),
                  pl.BlockSpec((rows, ym.shape[1]), lambda i: (i, 0)),
                  pl.BlockSpec((rows, yd.shape[1]), lambda i: (i, 0)),
                  full(woa), full(wob), full(g2), full(wqt), full(sk)],
        out_specs=(pl.BlockSpec((rows, d), lambda i: (i, 0)),
                   pl.BlockSpec((d, rows), lambda i: (0, i)),
                   gate_spec, gate_spec, gate_spec),
        out_shape=(jax.ShapeDtypeStruct((t, d), F32), jax.ShapeDtypeStruct((d, t), BF16),
                   gate, gate, gate),
        scratch_shapes=[pltpu.VMEM((PEER_TOPK, rows), F32), pltpu.VMEM((PEER_TOPK, rows), F32),
                        pltpu.VMEM((PEER_TOPK + 8 * SUBLANES, rows), F32)],
        compiler_params=pltpu.CompilerParams(dimension_semantics=("arbitrary",),
                                             vmem_limit_bytes=VMEM_LIMIT),
        name="front",
    )(x2, ym, yd, woa, wob, g2, wqt, sk)


def _dense_kernel(n2t_ref, u_ref, vt_ref, rho_ref, p_ref, r_ref, h1_ref, o_ref,
                  acc_ref, w_ref, pre_ref, brho_ref, bp_ref, wc_ref, vtc_ref):
    e = pl.program_id(1)
    n_tok = n2t_ref.shape[1]
    n_exp = u_ref.shape[0]
    piece = DENSE_PIECE
    n_piece = n_exp // piece
    first_keys = piece // PEER_KEYS
    n_i = n_exp // PEER_KEYS
    n_tc = n_tok // LANES
    assert n_tc % 2 == 0
    n_half = n_tc // 2
    first, second = range(0, n_half), range(n_half, n_tc)

    def lanes_of(ref, tcs, rows=slice(None)):
        return jnp.concatenate([ref[tc, rows, :] for tc in tcs], axis=1)

    @pl.when(e == 0)
    def _():
        acc_ref[...] = jnp.zeros_like(acc_ref)
        wc_ref[...] = jnp.zeros_like(wc_ref)
        vtc_ref[...] = jnp.zeros_like(vtc_ref)

    for h in range(PEER_HEADS):
        for ii in range(n_i):
            for tc in range(n_tc):
                brho_ref[h * n_i + ii, tc] = jnp.broadcast_to(rho_ref[h, tc, ii:ii + 1, :], (SUBLANES, LANES))
                bp_ref[h * n_i + ii, tc] = jnp.broadcast_to(p_ref[h, tc, ii:ii + 1, :], (SUBLANES, LANES))

    def pre_matmuls(tcs):
        cols = slice(tcs[0] * LANES, (tcs[-1] + 1) * LANES)
        for pc in range(n_piece):
            rows = slice(pc * piece, (pc + 1) * piece)
            val = _dot(u_ref[rows, :], n2t_ref[:, cols])
            for j, tc in enumerate(tcs):
                pre_ref[tc, rows, :] = val[:, j * LANES:(j + 1) * LANES]

    pre_matmuls(first)
    acc_ref[1] += _dot(vtc_ref[...], lanes_of(wc_ref, range(n_half)))
    pre_matmuls(second)

    def gate(pc, tcs):
        i0 = pc * first_keys
        nvr = GATE_ROWS // SUBLANES
        for tc in tcs:
            for jt in range(PEER_KEYS // GATE_ROWS):
                js = slice(jt * GATE_ROWS, (jt + 1) * GATE_ROWS)
                g = [None] * first_keys
                for h in range(PEER_HEADS):
                    r = r_ref[h, tc, js, :].reshape(nvr, SUBLANES, LANES)
                    for k in range(first_keys):
                        rho = brho_ref[h * n_i + i0 + k, tc][None]
                        p = bp_ref[h * n_i + i0 + k, tc][None]
                        term = jnp.where(r >= rho, p * r, 0.0)
                        g[k] = term if g[k] is None else g[k] + term
                for k in range(first_keys):
                    row0 = (i0 + k) * PEER_KEYS + jt * GATE_ROWS
                    pre = pre_ref[tc, row0:row0 + GATE_ROWS, :]
                    act = pre * (1.0 + lax.erf(pre * math.sqrt(0.5)))
                    w_ref[tc, row0:row0 + GATE_ROWS, :] = (g[k].reshape(GATE_ROWS, LANES) * act).astype(BF16)

    for pc in range(n_piece):
        gate(pc, first)
    acc_ref[0] += _dot(vt_ref[...], lanes_of(w_ref, first))
    for pc in range(n_piece):
        gate(pc, second)
    for j, tc in enumerate(second):
        wc_ref[j] = w_ref[tc]
    vtc_ref[...] = vt_ref[...]

    @pl.when(e == pl.num_programs(1) - 1)
    def _():
        acc_b = acc_ref[1] + _dot(vt_ref[...], lanes_of(wc_ref, range(n_half)))
        acc = jnp.concatenate([acc_ref[0], acc_b], axis=1)
        o_ref[...] = h1_ref[...] + acc.T


def _dense(n2t, u, vt, rho, p, r, h1, tokens, experts):
    d, t = n2t.shape
    ne = u.shape[0]
    tokens = min(tokens, t)
    grid = (t // tokens, ne // experts)
    n_i = experts // PEER_KEYS
    n_tc = tokens // LANES
    return pl.pallas_call(
        _dense_kernel, grid=grid,
        in_specs=[pl.BlockSpec((d, tokens), lambda ti, ei: (0, ti)),
                  pl.BlockSpec((experts, d), lambda ti, ei: (ei, 0)),
                  pl.BlockSpec((d, experts), lambda ti, ei: (0, ei)),
                  pl.BlockSpec((PEER_HEADS, n_tc, n_i, LANES), lambda ti, ei: (0, ti, ei, 0)),
                  pl.BlockSpec((PEER_HEADS, n_tc, n_i, LANES), lambda ti, ei: (0, ti, ei, 0)),
                  pl.BlockSpec((PEER_HEADS, n_tc, PEER_KEYS, LANES), lambda ti, ei: (0, ti, 0, 0)),
                  pl.BlockSpec((tokens, d), lambda ti, ei: (ti, 0))],
        out_specs=pl.BlockSpec((tokens, d), lambda ti, ei: (ti, 0)),
        out_shape=jax.ShapeDtypeStruct((t, d), F32),
        scratch_shapes=[pltpu.VMEM((2, d, tokens // 2), F32),
                        pltpu.VMEM((n_tc, experts, LANES), BF16),
                        pltpu.VMEM((n_tc, experts, LANES), F32),
                        pltpu.VMEM((PEER_HEADS * n_i, n_tc, SUBLANES, LANES), F32),
                        pltpu.VMEM((PEER_HEADS * n_i, n_tc, SUBLANES, LANES), F32),
                        pltpu.VMEM((n_tc // 2, experts, LANES), BF16),
                        pltpu.VMEM((d, experts), BF16)],
        compiler_params=pltpu.CompilerParams(dimension_semantics=("arbitrary", "arbitrary"),
                                             vmem_limit_bytes=VMEM_LIMIT),
        name="dense",
    )(n2t, u, vt, rho, p, r, h1)


def _rope_tables(pos):
    half = MLA_ROPE // 2
    inv_freq = ROPE_THETA ** (-jnp.arange(half, dtype=F32) / half)
    ang = pos.astype(F32)[:, None] * inv_freq[None, :]
    cos, sin = jnp.cos(ang), jnp.sin(ang)
    n = pos.shape[0]
    one = jnp.ones((n, MLA_NOPE), F32)
    zero = jnp.zeros((n, MLA_NOPE), F32)
    tail1 = jnp.ones((n, LANES - MLA_QK), F32)
    tail0 = jnp.zeros((n, LANES - MLA_QK), F32)
    zh = jnp.zeros((n, half), F32)
    cos_t = jnp.concatenate([one, cos, cos, tail1], axis=1)
    sina = jnp.concatenate([zero, -sin, zh, tail0], axis=1)
    sinb = jnp.concatenate([zero, zh, sin, tail0], axis=1)
    return cos_t, sina, sinb


def _bucket_np(dist):
    n = np.maximum(dist, 0)
    max_exact = NUM_BUCKETS // 2
    nf = np.maximum(n, 1).astype(np.float32)
    large = max_exact + (np.log(nf / np.float32(max_exact)) / np.float32(math.log(MAX_DISTANCE / max_exact))
                         * np.float32(NUM_BUCKETS - max_exact)).astype(np.int32)
    large = np.minimum(large, NUM_BUCKETS - 1)
    return np.where(n < max_exact, n, large)


def _bias_tiles(rel_bias, tq):
    far = int(np.min(np.nonzero(_bucket_np(np.arange(4 * MAX_DISTANCE)) == NUM_BUCKETS - 1)[0]))
    assert np.all(_bucket_np(np.arange(far, 8 * MAX_DISTANCE)) == NUM_BUCKETS - 1)
    if tq < far:
        raise NotImplementedError("attention tile smaller than the relative-bias window")
    table = rel_bias.astype(F32)
    shifted = (table - table[NUM_BUCKETS - 1][None, :]) * LOG2E

    def toeplitz(n_rows, n_cols, off):
        period = n_rows + n_cols - 1
        dist = np.arange(period) - (n_cols - 1) + off
        vals = jnp.where((dist >= 0)[:, None], shifted[_bucket_np(np.maximum(dist, 0))], NEG)
        u = vals[::-1].T
        flat = jnp.tile(u, (1, n_rows + 1))[:, :n_rows * (period + 1)]
        return flat.reshape(-1, n_rows, period + 1)[:, ::-1, :n_cols]

    diag = toeplitz(tq, tq, 0)
    prev = toeplitz(tq, tq, tq)
    pad = jnp.asarray(_mask_tiles(tq)[0][:, 0])
    nh = rel_bias.shape[1]
    meta0 = jnp.where(pad[0][None] < 0.0, NEG, toeplitz(tq, LANES, N_META))
    meta_far = jnp.broadcast_to(pad[1][None], (nh, tq, LANES))
    return jnp.stack([meta0, meta_far]), diag, prev


def _mask_tiles(tq):
    r = np.arange(tq)[:, None]
    c = np.arange(tq)[None, :]
    diag = np.where(r >= c, 0.0, NEG).astype(np.float32)[None]
    cm = np.arange(LANES)[None, :]
    meta = np.broadcast_to(np.where(cm < N_META, 0.0, NEG).astype(np.float32), (tq, LANES))
    return np.stack([meta, meta])[:, None], diag


def _pad_lanes(a, width):
    return jnp.pad(a, [(0, 0)] * (a.ndim - 1) + [(0, width - a.shape[-1])])


def kernel(x, meta_tokens, rel_bias, attn_norm, w_in, mla_q_norm, mla_w_uq, mla_kv_norm, mla_w_ukv,
           mla_qk_norm_q, mla_qk_norm_k, diff_q_norm, diff_k_norm, diff_lambda_q1, diff_lambda_k1,
           diff_lambda_q2, diff_lambda_k2, diff_subln, w_out, ffn_norm, peer_w_query, peer_sub_keys,
           peer_u, peer_v):
    b, s, d = x.shape
    assert attn_norm.shape[0] == 1, "meta rows are only used as keys: single layer"
    lambda_init = 0.8 - 0.6 * math.exp(-0.3 * 0)
    tq = min(ATTN_TILE, s)
    assert s % tq == 0

    wi = w_in[0]
    s1, s2, s3 = MLA_Q_RANK, MLA_Q_RANK + MLA_KV_RANK, MLA_Q_RANK + MLA_KV_RANK + MLA_ROPE
    wkr = jnp.pad(wi[:, s2:s3], ((0, 0), (MLA_NOPE, LANES - MLA_QK)))
    w_in_arr = jnp.concatenate([wi[:, :s2], wkr, wi[:, s3:]], axis=1).astype(BF16)
    wuq = _pad_lanes(mla_w_uq[0].reshape(MLA_Q_RANK, MLA_HEADS, MLA_QK), LANES).reshape(MLA_Q_RANK, -1).astype(BF16)
    wukv = mla_w_ukv[0].reshape(MLA_KV_RANK, MLA_HEADS, MLA_NOPE + MLA_V)
    wk = _pad_lanes(wukv[..., :MLA_NOPE], LANES).reshape(MLA_KV_RANK, -1)
    wv = _pad_lanes(wukv[..., MLA_NOPE:], LANES).reshape(MLA_KV_RANK, -1)
    wkv = jnp.concatenate([wk, wv], axis=1).astype(BF16)
    row = lambda a: a.reshape(1, -1).astype(F32)
    gq = _pad_lanes(row(mla_qk_norm_q[0]), LANES)
    gk = _pad_lanes(row(mla_qk_norm_k[0]), LANES)
    gdq = jnp.tile(row(diff_q_norm[0]), (1, 2))
    gdk = jnp.tile(row(diff_k_norm[0]), (1, 2))
    wts = (row(attn_norm[0]), w_in_arr, row(mla_q_norm[0]), wuq, row(mla_kv_norm[0]), wkv, gq, gk, gdq, gdk)

    pos_real = jnp.arange(N_META, N_META + s, dtype=jnp.int32)
    qm, km, vm, qd, kd, vd = _project(x, _rope_tables(pos_real), wts, PROJ_ROWS)
    pos_meta = jnp.arange(N_META, dtype=jnp.int32)
    _, km_m, vm_m, _, kd_m, vd_m = _project(meta_tokens[None].astype(x.dtype), _rope_tables(pos_meta), wts, N_META)
    padk = lambda a: jnp.pad(a[0], ((0, 0), (0, LANES - N_META), (0, 0)))

    mmeta, mdiag = (jnp.asarray(a) for a in _mask_tiles(tq))
    y_mla = _flash(qm[:, :, None], km, vm, padk(km_m), padk(vm_m), mmeta, mdiag, None, (),
                   diff=False, tq=tq, lambda_init=lambda_init)
    bmeta, bdiag, bprev = _bias_tiles(rel_bias, tq)
    extra = (row(diff_lambda_q1[0]), row(diff_lambda_k1[0]), row(diff_lambda_q2[0]), row(diff_lambda_k2[0]),
             row(diff_subln[0]))
    y_diff = _flash(qd, kd, vd, padk(kd_m), padk(vd_m), bmeta, bdiag, bprev, extra,
                    diff=True, tq=tq, lambda_init=lambda_init)

    t = b * s
    wo = w_out[0].astype(BF16)
    n_mla = MLA_HEADS * MLA_V
    wqt = peer_w_query[0].T.astype(BF16)
    sk = peer_sub_keys[0].reshape(PEER_HEADS * 2, PEER_KEYS, PEER_SUB).astype(BF16)
    h1, n2t, g_rho, g_p, g_r = _front(x.reshape(t, d), y_mla.reshape(t, -1), y_diff.reshape(t, -1),
                                      wo[:n_mla], wo[n_mla:], row(ffn_norm[0]), wqt, sk, FRONT_ROWS)
    u_bf = peer_u[0].astype(BF16)
    vt_bf = peer_v[0].T.astype(BF16)
    out = _dense(n2t, u_bf, vt_bf, g_rho, g_p, g_r, h1, DENSE_TOKENS, DENSE_EXPERTS)
    return out.reshape(b, s, d)
```

```python
import functools
import math

import numpy as np
import jax
import jax.numpy as jnp
from jax import lax
from jax.experimental import pallas as pl
from jax.experimental.pallas import tpu as pltpu

F32 = jnp.float32
BF16 = jnp.bfloat16

N_META = 16
EPS = 1e-6
LANES = 128
SUBLANES = 8
PACKED_ROWS = 16

MLA_HEADS = 8
MLA_Q_RANK = 256
MLA_KV_RANK = 256
MLA_NOPE = 64
MLA_ROPE = 32
MLA_V = 64
MLA_QK = MLA_NOPE + MLA_ROPE
ROPE_THETA = 10000.0

DIFF_HEADS = 4
DIFF_HD = 64
DIFF_V = 2 * DIFF_HD

NUM_BUCKETS = 32
MAX_DISTANCE = 128

PEER_HEADS = 8
PEER_KEYS = 128
PEER_TOPK = 16
PEER_SUB = 128

NEG = -1e30

PROJ_ROWS = 512
PROJ_SUB_ROWS = 256
ATTN_TILE = 512
ATTN_CHAIN_ROWS = 256
LOG2E = math.log2(math.e)
FRONT_ROWS = 256
DENSE_TOKENS = 512
DENSE_EXPERTS = 2048
DENSE_PIECE = 256
GATE_ROWS = 64
VMEM_LIMIT = 56 * 1024 * 1024


def _rms(x, g):
    return x * lax.rsqrt(jnp.mean(x * x, axis=-1, keepdims=True) + EPS) * g


def _dot(a, b):
    return jnp.dot(a, b, preferred_element_type=F32)


def _lane_tile(a, n):
    return a if n == 1 else jnp.concatenate([a] * n, axis=1)


def _dot_nt(a, b):
    return lax.dot_general(a, b, (((1,), (1,)), ((), ())), preferred_element_type=F32)


def _proj_kernel(x_ref, g1_ref, win_ref, gcq_ref, wuq_ref, gckv_ref, wkv_ref, gq_ref, gk_ref,
                 gdq_ref, gdk_ref, cos_ref, sina_ref, sinb_ref,
                 qm_ref, km_ref, vm_ref, qd_ref, kd_ref, vd_ref, *, scale_m, scale_d):
    rows = x_ref.shape[1]
    sub = min(PROJ_SUB_ROWS, rows)
    subs = [slice(i * sub, (i + 1) * sub) for i in range(rows // sub)]
    ns = [_rms(x_ref[0, sl, :], g1_ref[...]).astype(BF16) for sl in subs]
    projs = [_dot(n, win_ref[...]) for n in ns]
    cqs = [_rms(p[:, 0:256], gcq_ref[...]).astype(BF16) for p in projs]
    ckvs = [_rms(p[:, 256:512], gckv_ref[...]).astype(BF16) for p in projs]
    q_raws = [_dot(c, wuq_ref[...]) for c in cqs]
    kvs = [_dot(c, wkv_ref[...]) for c in ckvs]
    gq = gq_ref[...]
    gk = gk_ref[...]
    gdq = gdq_ref[...]
    gdk = gdk_ref[...]
    lo_v = lax.broadcasted_iota(jnp.int32, (1, LANES), 1) < MLA_V
    lo = lax.broadcasted_iota(jnp.int32, (1, LANES), 1) < DIFF_HD
    inv_qk = 1.0 / MLA_QK
    inv_hd = 1.0 / DIFF_HD

    def halfnorm(t, g):
        sq = t * t
        s_lo = jnp.sum(jnp.where(lo, sq, 0.0), axis=-1, keepdims=True)
        s_hi = jnp.sum(jnp.where(lo, 0.0, sq), axis=-1, keepdims=True)
        r = jnp.where(lo, lax.rsqrt(s_lo * inv_hd + EPS), lax.rsqrt(s_hi * inv_hd + EPS))
        return t * r * g

    for sl, proj, q_raw, kv in zip(subs, projs, q_raws, kvs):
        krp = proj[:, 512:640]
        dq = proj[:, 640:1152]
        dk = proj[:, 1152:1664]
        dv = proj[:, 1664:2176]
        cos = cos_ref[sl, :]
        sina = sina_ref[sl, :]
        sinb = sinb_ref[sl, :]

        def rope(t, cos=cos, sina=sina, sinb=sinb):
            return t * cos + pltpu.roll(t, LANES - 16, 1) * sina + pltpu.roll(t, 16, 1) * sinb

        for h in range(MLA_HEADS):
            hl = slice(h * LANES, (h + 1) * LANES)
            qh = q_raw[:, hl]
            rq = lax.rsqrt(jnp.sum(qh * qh, axis=-1, keepdims=True) * inv_qk + EPS)
            qm_ref[0, h, sl, :] = (rope(qh * rq * gq) * scale_m).astype(BF16)
            kh = kv[:, hl] + krp
            rk = lax.rsqrt(jnp.sum(kh * kh, axis=-1, keepdims=True) * inv_qk + EPS)
            km_ref[0, h, sl, :] = rope(kh * rk * gk).astype(BF16)
            vm_ref[0, h, sl, :] = jnp.where(lo_v, kv[:, 1024 + h * LANES:1024 + (h + 1) * LANES], 1.0).astype(BF16)
        for h in range(DIFF_HEADS):
            hl = slice(h * LANES, (h + 1) * LANES)
            qn = halfnorm(dq[:, hl], gdq) * scale_d
            qd_ref[0, h, 0, sl, :] = jnp.where(lo, qn, 0.0).astype(BF16)
            qd_ref[0, h, 1, sl, :] = jnp.where(lo, 0.0, qn).astype(BF16)
            kd_ref[0, h, sl, :] = halfnorm(dk[:, hl], gdk).astype(BF16)
            vd_ref[0, h, sl, 0:LANES] = dv[:, hl].astype(BF16)
            vd_ref[0, h, sl, LANES:2 * LANES] = jnp.ones_like(dv[:, hl]).astype(BF16)


def _project(x3, tabs, wts, rows):
    b, s, d = x3.shape
    rows = min(rows, s)
    assert s % rows == 0
    grid = (b, s // rows)
    full = lambda a: pl.BlockSpec(a.shape, lambda i, j: (0,) * a.ndim)
    tab_spec = pl.BlockSpec((rows, LANES), lambda i, j: (j, 0))
    in_specs = [pl.BlockSpec((1, rows, d), lambda i, j: (i, j, 0))] + [full(w) for w in wts] + [tab_spec] * 3
    hs = lambda nh: pl.BlockSpec((1, nh, rows, LANES), lambda i, j: (i, 0, j, 0))
    out_shape = (
        jax.ShapeDtypeStruct((b, MLA_HEADS, s, LANES), BF16),
        jax.ShapeDtypeStruct((b, MLA_HEADS, s, LANES), BF16),
        jax.ShapeDtypeStruct((b, MLA_HEADS, s, LANES), BF16),
        jax.ShapeDtypeStruct((b, DIFF_HEADS, 2, s, LANES), BF16),
        jax.ShapeDtypeStruct((b, DIFF_HEADS, s, LANES), BF16),
        jax.ShapeDtypeStruct((b, DIFF_HEADS, s, 2 * LANES), BF16),
    )
    out_specs = (hs(MLA_HEADS), hs(MLA_HEADS), hs(MLA_HEADS),
                 pl.BlockSpec((1, DIFF_HEADS, 2, rows, LANES), lambda i, j: (i, 0, 0, j, 0)),
                 hs(DIFF_HEADS),
                 pl.BlockSpec((1, DIFF_HEADS, rows, 2 * LANES), lambda i, j: (i, 0, j, 0)))
    kern = functools.partial(_proj_kernel, scale_m=MLA_QK ** -0.5 * LOG2E, scale_d=DIFF_HD ** -0.5 * LOG2E)
    return pl.pallas_call(
        kern, grid=grid, in_specs=in_specs, out_specs=out_specs, out_shape=out_shape,
        compiler_params=pltpu.CompilerParams(dimension_semantics=("arbitrary", "arbitrary"),
                                             vmem_limit_bytes=VMEM_LIMIT),
        name="proj",
    )(x3, *wts, *tabs)


def _flash_kernel(*refs, hp, nm, tq, rc, diff, lambda_init):
    if diff:
        (q_ref, k_ref, v_ref, kmeta_ref, vmeta_ref, bmeta_ref, bdiag_ref, bprev_ref,
         lq1_ref, lk1_ref, lq2_ref, lk2_ref, subln_ref, o_ref, m_ref, acc_ref) = refs
    else:
        q_ref, k_ref, v_ref, kmeta_ref, vmeta_ref, bmeta_ref, bdiag_ref, o_ref, m_ref, acc_ref = refs
        bprev_ref = None
    qi = pl.program_id(2)
    nchunk = tq // rc
    nv = acc_ref.shape[1]
    chains = [(hh, mm, c) for hh in range(hp) for mm in range(nm) for c in range(nchunk)]
    qs = [q_ref[0, hh, mm, c * rc:(c + 1) * rc, :] for (hh, mm, c) in chains]

    def update(i, s, vt, first=False):
        rows = slice(i * rc, (i + 1) * rc)
        m_cur = jnp.max(s, axis=-1, keepdims=True)
        if first:
            m_new = jnp.broadcast_to(m_cur, (rc, LANES))
        else:
            m_prev = m_ref[rows]
            m_new = jnp.maximum(m_prev, m_cur)
        p = jnp.exp2((s - _lane_tile(m_new, s.shape[1] // LANES)).astype(BF16))
        pv = _dot(p, vt)
        if first:
            acc_ref[rows] = pv
        else:
            alpha = jnp.exp2(m_prev - m_new)
            acc_ref[rows] = _lane_tile(alpha, nv // LANES) * acc_ref[rows] + pv
        m_ref[rows] = m_new

    meta_scores = [_dot_nt(qs[i], kmeta_ref[hh]) + bmeta_ref[0, 0, c * rc:(c + 1) * rc, :]
                   for i, (hh, mm, c) in enumerate(chains)]
    for i, (hh, mm, c) in enumerate(chains):
        update(i, meta_scores[i], vmeta_ref[hh], first=True)

    def step(j, bias_ref=None):
        off = pl.multiple_of(j * tq, tq)
        scores = []
        for i, (hh, mm, c) in enumerate(chains):
            s = _dot_nt(qs[i], k_ref[0, hh, pl.ds(off, tq), :])
            if bias_ref is not None:
                s = s + bias_ref[0, c * rc:(c + 1) * rc, :]
            scores.append(s)
        for i, (hh, mm, c) in enumerate(chains):
            update(i, scores[i], v_ref[0, hh, pl.ds(off, tq), :])

    def plain(j, carry):
        step(j)
        return carry

    if diff:
        lax.fori_loop(0, jnp.maximum(qi - 1, 0), plain, 0)
        pl.when(qi > 0)(lambda: step(qi - 1, bprev_ref))
    else:
        lax.fori_loop(0, qi, plain, 0)
    step(qi, bdiag_ref)

    def normalized(i):
        acc = acc_ref[i * rc:(i + 1) * rc]
        return acc[:, :LANES] / acc[:, nv - 1:nv]

    outs = {ch: normalized(i) for i, ch in enumerate(chains)}
    if diff:
        lam = (jnp.exp(jnp.sum(lq1_ref[...] * lk1_ref[...], axis=-1, keepdims=True))
               - jnp.exp(jnp.sum(lq2_ref[...] * lk2_ref[...], axis=-1, keepdims=True)) + lambda_init)
        for c in range(nchunk):
            d = outs[(0, 0, c)] - lam * outs[(0, 1, c)]
            y = _rms(d, subln_ref[...]) * (1.0 - lambda_init)
            o_ref[0, c * rc:(c + 1) * rc, :] = y.astype(BF16)
    else:
        lo = lax.broadcasted_iota(jnp.int32, (1, LANES), 1) < MLA_V
        for c in range(nchunk):
            pair = jnp.where(lo, outs[(0, 0, c)], pltpu.roll(outs[(1, 0, c)], MLA_V, 1))
            o_ref[0, c * rc:(c + 1) * rc, :] = pair.astype(BF16)


def _flash(q, k, v, kmeta, vmeta, bmeta, bdiag, bprev, extra, *, diff, tq, lambda_init):
    b, g, nm, s, _ = q.shape
    nv = v.shape[-1]
    hp = 1 if diff else 2
    ngrid = g // hp
    nq = s // tq
    rows = tq
    rc = min(ATTN_CHAIN_ROWS, tq)
    grid = (b, ngrid, nq)
    gsel = (lambda gi: gi) if diff else (lambda gi: 0)
    in_specs = [
        pl.BlockSpec((1, hp, nm, tq, LANES), lambda bi, gi, qi: (bi, gi, 0, qi, 0)),
        pl.BlockSpec((1, hp, s, LANES), lambda bi, gi, qi: (bi, gi, 0, 0)),
        pl.BlockSpec((1, hp, s, nv), lambda bi, gi, qi: (bi, gi, 0, 0)),
        pl.BlockSpec((hp, LANES, LANES), lambda bi, gi, qi: (gi, 0, 0)),
        pl.BlockSpec((hp, LANES, nv), lambda bi, gi, qi: (gi, 0, 0)),
        pl.BlockSpec((1, 1, rows, LANES), lambda bi, gi, qi: (jnp.minimum(qi, 1), gsel(gi), 0, 0)),
        pl.BlockSpec((1, rows, tq), lambda bi, gi, qi: (gsel(gi), 0, 0)),
    ]
    args = [q, k, v, kmeta, vmeta, bmeta, bdiag]
    if diff:
        in_specs.append(pl.BlockSpec((1, rows, tq), lambda bi, gi, qi: (gi, 0, 0)))
        args.append(bprev)
        for e in extra:
            in_specs.append(pl.BlockSpec(e.shape, lambda bi, gi, qi: (0, 0)))
            args.append(e)
    kern = functools.partial(_flash_kernel, hp=hp, nm=nm, tq=tq, rc=rc, diff=diff, lambda_init=lambda_init)
    return pl.pallas_call(
        kern, grid=grid, in_specs=in_specs,
        out_specs=pl.BlockSpec((1, tq, LANES), lambda bi, gi, qi: (bi, qi, gi)),
        out_shape=jax.ShapeDtypeStruct((b, s, ngrid * LANES), BF16),
        scratch_shapes=[pltpu.VMEM((hp * nm * tq, LANES), F32), pltpu.VMEM((hp * nm * tq, nv), F32)],
        compiler_params=pltpu.CompilerParams(dimension_semantics=("arbitrary",) * 3,
                                             vmem_limit_bytes=VMEM_LIMIT),
        name="flash_diff" if diff else "flash_mla",
    )(*args)


def _top_values(s, n, row_ref=None):
    vals = []
    for i in range(n):
        m = jnp.max(s, axis=0, keepdims=True)
        vals.append(m)
        if row_ref is not None and i < PEER_TOPK:
            row_ref[i:i + 1, :] = m
        if i + 1 < n:
            s = jnp.where(s == m, -jnp.inf, s)
    return vals


def _front_kernel(x_ref, ym_ref, yd_ref, woa_ref, wob_ref, g2_ref, wqt_ref, sk_ref,
                  h1_ref, n2t_ref, rho_ref, p_ref, r_ref, v1s_ref, v2s_ref, cand_ref):
    h1 = x_ref[...] + _dot(ym_ref[...], woa_ref[...]) + _dot(yd_ref[...], wob_ref[...])
    h1_ref[...] = h1
    n2t = _rms(h1, g2_ref[...]).T.astype(BF16)
    n2t_ref[...] = n2t
    qpt = _dot(wqt_ref[...], n2t).astype(BF16)
    k = PEER_TOPK
    for h in range(PEER_HEADS):
        s1 = _dot(sk_ref[2 * h], qpt[(2 * h) * PEER_SUB:(2 * h + 1) * PEER_SUB])
        s2 = _dot(sk_ref[2 * h + 1], qpt[(2 * h + 1) * PEER_SUB:(2 * h + 2) * PEER_SUB])
        v1 = _top_values(s1, k + 1, v1s_ref)
        v2 = _top_values(s2, k + 1, v2s_ref)
        v2lo = v2s_ref[0:SUBLANES, :]
        row = lax.broadcasted_iota(jnp.int32, v2lo.shape, 0)
        cand_ref[0:k, :] = v1[0] + v2s_ref[...]
        cand_ref[k:k + SUBLANES, :] = v1[1] + v2lo
        for a in range(2, SUBLANES):
            lo = k + (a - 1) * SUBLANES
            cand_ref[lo:lo + SUBLANES, :] = jnp.where(row < (k + 1) // (a + 1), v1[a] + v2lo, -jnp.inf)
        cand_ref[k + 7 * SUBLANES:k + 8 * SUBLANES, :] = v1s_ref[SUBLANES:k, :] + v2[0]
        c = _top_values(cand_ref[...], k + 1)
        c17 = jnp.maximum(c[k], jnp.maximum(v1[k] + v2[0], v1[0] + v2[k]))
        tau = 0.5 * (c[k - 1] + c17)
        z = jnp.exp(c[0] - c[0])
        for i in range(1, k):
            z = z + jnp.exp(c[i] - c[0])
        e2 = [jnp.exp(v2[l] - v2[0]) for l in range(k)]
        n_all = 4
        rho = jnp.full(s1.shape, jnp.inf, F32)
        for l in range(n_all):
            rho = jnp.where(s1 + v2[l] >= tau, e2[l], rho)
        for a in range(k // (n_all + 1)):
            rho_a = jnp.full(tau.shape, jnp.inf, F32)
            for l in range(k // (a + 1)):
                rho_a = jnp.where(v1[a] + v2[l] >= tau, e2[l], rho_a)
            rho = jnp.where(s1 == v1[a], rho_a, rho)
        p = jnp.exp(s1 - v1[0]) * (math.sqrt(0.5) / z)
        r = jnp.exp(s2 - v2[0])
        for c in range(s1.shape[1] // LANES):
            cs = slice(c * LANES, (c + 1) * LANES)
            rho_ref[h, c] = rho[:, cs]
            p_ref[h, c] = p[:, cs]
            r_ref[h, c] = r[:, cs]


def _front(x2, ym, yd, woa, wob, g2, wqt, sk, rows):
    t, d = x2.shape
    rows = min(rows, t)
    grid = (t // rows,)
    full = lambda a: pl.BlockSpec(a.shape, lambda i: (0,) * a.ndim)
    assert rows % LANES == 0
    gate = jax.ShapeDtypeStruct((PEER_HEADS, t // LANES, PEER_KEYS, LANES), F32)
    gate_spec = pl.BlockSpec((PEER_HEADS, rows // LANES, PEER_KEYS, LANES), lambda i: (0, i, 0, 0))
    return pl.pallas_call(
        _front_kernel, grid=grid,
        in_specs=[pl.BlockSpec((rows, d), lambda i: (i, 0)),
                  pl.BlockSpec((rows, ym.shape[1]), lambda i: (i, 0)),
                  pl.BlockSpec((rows, yd.shape[1]), lambda i: (i, 0)),
                  full(woa), full(wob), full(g2), full(wqt), full(sk)],
        out_specs=(pl.BlockSpec((rows, d), lambda i: (i, 0)),
                   pl.BlockSpec((d, rows), lambda i: (0, i)),
                   gate_spec, gate_spec, gate_spec),
        out_shape=(jax.ShapeDtypeStruct((t, d), F32), jax.ShapeDtypeStruct((d, t), BF16),
                   gate, gate, gate),
        scratch_shapes=[pltpu.VMEM((PEER_TOPK, rows), F32), pltpu.VMEM((PEER_TOPK, rows), F32),
                        pltpu.VMEM((PEER_TOPK + 8 * SUBLANES, rows), F32)],
        compiler_params=pltpu.CompilerParams(dimension_semantics=("arbitrary",),
                                             vmem_limit_bytes=VMEM_LIMIT),
        name="front",
    )(x2, ym, yd, woa, wob, g2, wqt, sk)


def _dense_kernel(n2t_ref, u_ref, vt_ref, rho_ref, p_ref, r_ref, h1_ref, o_ref,
                  acc_ref, w_ref, pre_ref, brho_ref, bp_ref, wc_ref, vtc_ref):
    e = pl.program_id(1)
    n_tok = n2t_ref.shape[1]
    n_exp = u_ref.shape[0]
    piece = DENSE_PIECE
    n_piece = n_exp // piece
    first_keys = piece // PEER_KEYS
    n_i = n_exp // PEER_KEYS
    n_tc = n_tok // LANES
    assert n_tc % 2 == 0
    n_half = n_tc // 2
    first, second = range(0, n_half), range(n_half, n_tc)

    def lanes_of(ref, tcs, rows=slice(None)):
        return jnp.concatenate([ref[tc, rows, :] for tc in tcs], axis=1)

    @pl.when(e == 0)
    def _():
        acc_ref[...] = jnp.zeros_like(acc_ref)
        wc_ref[...] = jnp.zeros_like(wc_ref)
        vtc_ref[...] = jnp.zeros_like(vtc_ref)

    for h in range(PEER_HEADS):
        for ii in range(n_i):
            for tc in range(n_tc):
                brho_ref[h * n_i + ii, tc] = jnp.broadcast_to(rho_ref[h, tc, ii:ii + 1, :], (SUBLANES, LANES))
                bp_ref[h * n_i + ii, tc] = jnp.broadcast_to(p_ref[h, tc, ii:ii + 1, :], (SUBLANES, LANES))

    def pre_matmuls(tcs):
        cols = slice(tcs[0] * LANES, (tcs[-1] + 1) * LANES)
        for pc in range(n_piece):
            rows = slice(pc * piece, (pc + 1) * piece)
            val = _dot(u_ref[rows, :], n2t_ref[:, cols])
            for j, tc in enumerate(tcs):
                pre_ref[tc, rows, :] = val[:, j * LANES:(j + 1) * LANES]

    pre_matmuls(first)
    acc_ref[1] += _dot(vtc_ref[...], lanes_of(wc_ref, range(n_half)))
    pre_matmuls(second)

    def gate(pc, tcs):
        i0 = pc * first_keys
        nvr = GATE_ROWS // SUBLANES
        for tc in tcs:
            for jt in range(PEER_KEYS // GATE_ROWS):
                js = slice(jt * GATE_ROWS, (jt + 1) * GATE_ROWS)
                g = [None] * first_keys
                for h in range(PEER_HEADS):
                    r = r_ref[h, tc, js, :].reshape(nvr, SUBLANES, LANES)
                    for k in range(first_keys):
                        rho = brho_ref[h * n_i + i0 + k, tc][None]
                        p = bp_ref[h * n_i + i0 + k, tc][None]
                        term = jnp.where(r >= rho, p * r, 0.0)
                        g[k] = term if g[k] is None else g[k] + term
                for k in range(first_keys):
                    row0 = (i0 + k) * PEER_KEYS + jt * GATE_ROWS
                    pre = pre_ref[tc, row0:row0 + GATE_ROWS, :]
                    act = pre * (1.0 + lax.erf(pre))
                    w_ref[tc, row0:row0 + GATE_ROWS, :] = (g[k].reshape(GATE_ROWS, LANES) * act).astype(BF16)

    for pc in range(n_piece):
        gate(pc, first)
    acc_ref[0] += _dot(vt_ref[...], lanes_of(w_ref, first))
    for pc in range(n_piece):
        gate(pc, second)
    for j, tc in enumerate(second):
        wc_ref[j] = w_ref[tc]
    vtc_ref[...] = vt_ref[...]

    @pl.when(e == pl.num_programs(1) - 1)
    def _():
        acc_b = acc_ref[1] + _dot(vt_ref[...], lanes_of(wc_ref, range(n_half)))
        acc = jnp.concatenate([acc_ref[0], acc_b], axis=1)
        o_ref[...] = h1_ref[...] + acc.T


def _dense(n2t, u, vt, rho, p, r, h1, tokens, experts):
    d, t = n2t.shape
    ne = u.shape[0]
    tokens = min(tokens, t)
    grid = (t // tokens, ne // experts)
    n_i = experts // PEER_KEYS
    n_tc = tokens // LANES
    return pl.pallas_call(
        _dense_kernel, grid=grid,
        in_specs=[pl.BlockSpec((d, tokens), lambda ti, ei: (0, ti)),
                  pl.BlockSpec((experts, d), lambda ti, ei: (ei, 0)),
                  pl.BlockSpec((d, experts), lambda ti, ei: (0, ei)),
                  pl.BlockSpec((PEER_HEADS, n_tc, n_i, LANES), lambda ti, ei: (0, ti, ei, 0)),
                  pl.BlockSpec((PEER_HEADS, n_tc, n_i, LANES), lambda ti, ei: (0, ti, ei, 0)),
                  pl.BlockSpec((PEER_HEADS, n_tc, PEER_KEYS, LANES), lambda ti, ei: (0, ti, 0, 0)),
                  pl.BlockSpec((tokens, d), lambda ti, ei: (ti, 0))],
        out_specs=pl.BlockSpec((tokens, d), lambda ti, ei: (ti, 0)),
        out_shape=jax.ShapeDtypeStruct((t, d), F32),
        scratch_shapes=[pltpu.VMEM((2, d, tokens // 2), F32),
                        pltpu.VMEM((n_tc, experts, LANES), BF16),
                        pltpu.VMEM((n_tc, experts, LANES), F32),
                        pltpu.VMEM((PEER_HEADS * n_i, n_tc, SUBLANES, LANES), F32),
                        pltpu.VMEM((PEER_HEADS * n_i, n_tc, SUBLANES, LANES), F32),
                        pltpu.VMEM((n_tc // 2, experts, LANES), BF16),
                        pltpu.VMEM((d, experts), BF16)],
        compiler_params=pltpu.CompilerParams(dimension_semantics=("arbitrary", "arbitrary"),
                                             vmem_limit_bytes=VMEM_LIMIT),
        name="dense",
    )(n2t, u, vt, rho, p, r, h1)


def _rope_tables(pos):
    half = MLA_ROPE // 2
    inv_freq = ROPE_THETA ** (-jnp.arange(half, dtype=F32) / half)
    ang = pos.astype(F32)[:, None] * inv_freq[None, :]
    cos, sin = jnp.cos(ang), jnp.sin(ang)
    n = pos.shape[0]
    one = jnp.ones((n, MLA_NOPE), F32)
    zero = jnp.zeros((n, MLA_NOPE), F32)
    tail1 = jnp.ones((n, LANES - MLA_QK), F32)
    tail0 = jnp.zeros((n, LANES - MLA_QK), F32)
    zh = jnp.zeros((n, half), F32)
    cos_t = jnp.concatenate([one, cos, cos, tail1], axis=1)
    sina = jnp.concatenate([zero, -sin, zh, tail0], axis=1)
    sinb = jnp.concatenate([zero, zh, sin, tail0], axis=1)
    return cos_t, sina, sinb


def _bucket_np(dist):
    n = np.maximum(dist, 0)
    max_exact = NUM_BUCKETS // 2
    nf = np.maximum(n, 1).astype(np.float32)
    large = max_exact + (np.log(nf / np.float32(max_exact)) / np.float32(math.log(MAX_DISTANCE / max_exact))
                         * np.float32(NUM_BUCKETS - max_exact)).astype(np.int32)
    large = np.minimum(large, NUM_BUCKETS - 1)
    return np.where(n < max_exact, n, large)


def _bias_tiles(rel_bias, tq):
    far = int(np.min(np.nonzero(_bucket_np(np.arange(4 * MAX_DISTANCE)) == NUM_BUCKETS - 1)[0]))
    assert np.all(_bucket_np(np.arange(far, 8 * MAX_DISTANCE)) == NUM_BUCKETS - 1)
    if tq < far:
        raise NotImplementedError("attention tile smaller than the relative-bias window")
    table = rel_bias.astype(F32)
    shifted = (table - table[NUM_BUCKETS - 1][None, :]) * LOG2E

    def toeplitz(n_rows, n_cols, off):
        period = n_rows + n_cols - 1
        dist = np.arange(period) - (n_cols - 1) + off
        vals = jnp.where((dist >= 0)[:, None], shifted[_bucket_np(np.maximum(dist, 0))], NEG)
        u = vals[::-1].T
        flat = jnp.tile(u, (1, n_rows + 1))[:, :n_rows * (period + 1)]
        return flat.reshape(-1, n_rows, period + 1)[:, ::-1, :n_cols]

    diag = toeplitz(tq, tq, 0)
    prev = toeplitz(tq, tq, tq)
    pad = jnp.asarray(_mask_tiles(tq)[0][:, 0])
    nh = rel_bias.shape[1]
    meta0 = jnp.where(pad[0][None] < 0.0, NEG, toeplitz(tq, LANES, N_META))
    meta_far = jnp.broadcast_to(pad[1][None], (nh, tq, LANES))
    return jnp.stack([meta0, meta_far]), diag, prev


def _mask_tiles(tq):
    r = np.arange(tq)[:, None]
    c = np.arange(tq)[None, :]
    diag = np.where(r >= c, 0.0, NEG).astype(np.float32)[None]
    cm = np.arange(LANES)[None, :]
    meta = np.broadcast_to(np.where(cm < N_META, 0.0, NEG).astype(np.float32), (tq, LANES))
    return np.stack([meta, meta])[:, None], diag


def _pad_lanes(a, width):
    return jnp.pad(a, [(0, 0)] * (a.ndim - 1) + [(0, width - a.shape[-1])])


def kernel(x, meta_tokens, rel_bias, attn_norm, w_in, mla_q_norm, mla_w_uq, mla_kv_norm, mla_w_ukv,
           mla_qk_norm_q, mla_qk_norm_k, diff_q_norm, diff_k_norm, diff_lambda_q1, diff_lambda_k1,
           diff_lambda_q2, diff_lambda_k2, diff_subln, w_out, ffn_norm, peer_w_query, peer_sub_keys,
           peer_u, peer_v):
    b, s, d = x.shape
    assert attn_norm.shape[0] == 1, "meta rows are only used as keys: single layer"
    lambda_init = 0.8 - 0.6 * math.exp(-0.3 * 0)
    tq = min(ATTN_TILE, s)
    assert s % tq == 0

    wi = w_in[0]
    s1, s2, s3 = MLA_Q_RANK, MLA_Q_RANK + MLA_KV_RANK, MLA_Q_RANK + MLA_KV_RANK + MLA_ROPE
    wkr = jnp.pad(wi[:, s2:s3], ((0, 0), (MLA_NOPE, LANES - MLA_QK)))
    w_in_arr = jnp.concatenate([wi[:, :s2], wkr, wi[:, s3:]], axis=1).astype(BF16)
    wuq = _pad_lanes(mla_w_uq[0].reshape(MLA_Q_RANK, MLA_HEADS, MLA_QK), LANES).reshape(MLA_Q_RANK, -1).astype(BF16)
    wukv = mla_w_ukv[0].reshape(MLA_KV_RANK, MLA_HEADS, MLA_NOPE + MLA_V)
    wk = _pad_lanes(wukv[..., :MLA_NOPE], LANES).reshape(MLA_KV_RANK, -1)
    wv = _pad_lanes(wukv[..., MLA_NOPE:], LANES).reshape(MLA_KV_RANK, -1)
    wkv = jnp.concatenate([wk, wv], axis=1).astype(BF16)
    row = lambda a: a.reshape(1, -1).astype(F32)
    gq = _pad_lanes(row(mla_qk_norm_q[0]), LANES)
    gk = _pad_lanes(row(mla_qk_norm_k[0]), LANES)
    gdq = jnp.tile(row(diff_q_norm[0]), (1, 2))
    gdk = jnp.tile(row(diff_k_norm[0]), (1, 2))
    wts = (row(attn_norm[0]), w_in_arr, row(mla_q_norm[0]), wuq, row(mla_kv_norm[0]), wkv, gq, gk, gdq, gdk)

    pos_real = jnp.arange(N_META, N_META + s, dtype=jnp.int32)
    qm, km, vm, qd, kd, vd = _project(x, _rope_tables(pos_real), wts, PROJ_ROWS)
    pos_meta = jnp.arange(N_META, dtype=jnp.int32)
    _, km_m, vm_m, _, kd_m, vd_m = _project(meta_tokens[None].astype(x.dtype), _rope_tables(pos_meta), wts, N_META)
    padk = lambda a: jnp.pad(a[0], ((0, 0), (0, LANES - N_META), (0, 0)))

    mmeta, mdiag = (jnp.asarray(a) for a in _mask_tiles(tq))
    y_mla = _flash(qm[:, :, None], km, vm, padk(km_m), padk(vm_m), mmeta, mdiag, None, (),
                   diff=False, tq=tq, lambda_init=lambda_init)
    bmeta, bdiag, bprev = _bias_tiles(rel_bias, tq)
    extra = (row(diff_lambda_q1[0]), row(diff_lambda_k1[0]), row(diff_lambda_q2[0]), row(diff_lambda_k2[0]),
             row(diff_subln[0]))
    y_diff = _flash(qd, kd, vd, padk(kd_m), padk(vd_m), bmeta, bdiag, bprev, extra,
                    diff=True, tq=tq, lambda_init=lambda_init)

    t = b * s
    wo = w_out[0].astype(BF16)
    n_mla = MLA_HEADS * MLA_V
    wqt = peer_w_query[0].T.astype(BF16)
    sk = peer_sub_keys[0].reshape(PEER_HEADS * 2, PEER_KEYS, PEER_SUB).astype(BF16)
    h1, n2t, g_rho, g_p, g_r = _front(x.reshape(t, d), y_mla.reshape(t, -1), y_diff.reshape(t, -1),
                                      wo[:n_mla], wo[n_mla:], row(ffn_norm[0]), wqt, sk, FRONT_ROWS)
    u_bf = (peer_u[0] * math.sqrt(0.5)).astype(BF16)
    vt_bf = peer_v[0].T.astype(BF16)
    out = _dense(n2t, u_bf, vt_bf, g_rho, g_p, g_r, h1, DENSE_TOKENS, DENSE_EXPERTS)
    return out.reshape(b, s, d)
```

```python
import functools
import math

import numpy as np
import jax
import jax.numpy as jnp
from jax import lax
from jax.experimental import pallas as pl
from jax.experimental.pallas import tpu as pltpu

F32 = jnp.float32
BF16 = jnp.bfloat16

N_META = 16
EPS = 1e-6
LANES = 128
SUBLANES = 8
PACKED_ROWS = 16

MLA_HEADS = 8
MLA_Q_RANK = 256
MLA_KV_RANK = 256
MLA_NOPE = 64
MLA_ROPE = 32
MLA_V = 64
MLA_QK = MLA_NOPE + MLA_ROPE
ROPE_THETA = 10000.0

DIFF_HEADS = 4
DIFF_HD = 64
DIFF_V = 2 * DIFF_HD

NUM_BUCKETS = 32
MAX_DISTANCE = 128

PEER_HEADS = 8
PEER_KEYS = 128
PEER_TOPK = 16
PEER_SUB = 128

NEG = -1e30

PROJ_ROWS = 512
PROJ_SUB_ROWS = 256
ATTN_TILE = 512
ATTN_CHAIN_ROWS = 256
LOG2E = math.log2(math.e)
FRONT_ROWS = 256
DENSE_TOKENS = 512
DENSE_EXPERTS = 2048
DENSE_PIECE = 256
GATE_ROWS = 64
VMEM_LIMIT = 56 * 1024 * 1024


def _rms(x, g):
    return x * lax.rsqrt(jnp.mean(x * x, axis=-1, keepdims=True) + EPS) * g


def _dot(a, b):
    return jnp.dot(a, b, preferred_element_type=F32)


def _lane_tile(a, n):
    return a if n == 1 else jnp.concatenate([a] * n, axis=1)


def _dot_nt(a, b):
    return lax.dot_general(a, b, (((1,), (1,)), ((), ())), preferred_element_type=F32)


def _proj_kernel(x_ref, g1_ref, win_ref, gcq_ref, wuq_ref, gckv_ref, wkv_ref, gq_ref, gk_ref,
                 gdq_ref, gdk_ref, cos_ref, sina_ref, sinb_ref,
                 qm_ref, km_ref, vm_ref, qd_ref, kd_ref, vd_ref, *, scale_m, scale_d):
    rows = x_ref.shape[1]
    sub = min(PROJ_SUB_ROWS, rows)
    subs = [slice(i * sub, (i + 1) * sub) for i in range(rows // sub)]
    ns = [_rms(x_ref[0, sl, :], g1_ref[...]).astype(BF16) for sl in subs]
    projs = [_dot(n, win_ref[...]) for n in ns]
    cqs = [_rms(p[:, 0:256], gcq_ref[...]).astype(BF16) for p in projs]
    ckvs = [_rms(p[:, 256:512], gckv_ref[...]).astype(BF16) for p in projs]
    q_raws = [_dot(c, wuq_ref[...]) for c in cqs]
    kvs = [_dot(c, wkv_ref[...]) for c in ckvs]
    gq = gq_ref[...]
    gk = gk_ref[...]
    gdq = gdq_ref[...]
    gdk = gdk_ref[...]
    lo_v = lax.broadcasted_iota(jnp.int32, (1, LANES), 1) < MLA_V
    lo = lax.broadcasted_iota(jnp.int32, (1, LANES), 1) < DIFF_HD
    inv_qk = 1.0 / MLA_QK
    inv_hd = 1.0 / DIFF_HD

    def halfnorm(t, g):
        sq = t * t
        s_lo = jnp.sum(jnp.where(lo, sq, 0.0), axis=-1, keepdims=True)
        s_hi = jnp.sum(jnp.where(lo, 0.0, sq), axis=-1, keepdims=True)
        r = jnp.where(lo, lax.rsqrt(s_lo * inv_hd + EPS), lax.rsqrt(s_hi * inv_hd + EPS))
        return t * r * g

    for sl, proj, q_raw, kv in zip(subs, projs, q_raws, kvs):
        krp = proj[:, 512:640]
        dq = proj[:, 640:1152]
        dk = proj[:, 1152:1664]
        dv = proj[:, 1664:2176]
        cos = cos_ref[sl, :]
        sina = sina_ref[sl, :]
        sinb = sinb_ref[sl, :]

        def rope(t, cos=cos, sina=sina, sinb=sinb):
            return t * cos + pltpu.roll(t, LANES - 16, 1) * sina + pltpu.roll(t, 16, 1) * sinb

        for h in range(MLA_HEADS):
            hl = slice(h * LANES, (h + 1) * LANES)
            qh = q_raw[:, hl]
            rq = lax.rsqrt(jnp.sum(qh * qh, axis=-1, keepdims=True) * inv_qk + EPS)
            qm_ref[0, h, sl, :] = (rope(qh * rq * gq) * scale_m).astype(BF16)
            kh = kv[:, hl] + krp
            rk = lax.rsqrt(jnp.sum(kh * kh, axis=-1, keepdims=True) * inv_qk + EPS)
            km_ref[0, h, sl, :] = rope(kh * rk * gk).astype(BF16)
            vm_ref[0, h, sl, :] = jnp.where(lo_v, kv[:, 1024 + h * LANES:1024 + (h + 1) * LANES], 1.0).astype(BF16)
        for h in range(DIFF_HEADS):
            hl = slice(h * LANES, (h + 1) * LANES)
            qn = halfnorm(dq[:, hl], gdq) * scale_d
            qd_ref[0, h, 0, sl, :] = jnp.where(lo, qn, 0.0).astype(BF16)
            qd_ref[0, h, 1, sl, :] = jnp.where(lo, 0.0, qn).astype(BF16)
            kd_ref[0, h, sl, :] = halfnorm(dk[:, hl], gdk).astype(BF16)
            vd_ref[0, h, sl, 0:LANES] = dv[:, hl].astype(BF16)
            vd_ref[0, h, sl, LANES:2 * LANES] = jnp.ones_like(dv[:, hl]).astype(BF16)


def _project(x3, tabs, wts, rows):
    b, s, d = x3.shape
    rows = min(rows, s)
    assert s % rows == 0
    grid = (b, s // rows)
    full = lambda a: pl.BlockSpec(a.shape, lambda i, j: (0,) * a.ndim)
    tab_spec = pl.BlockSpec((rows, LANES), lambda i, j: (j, 0))
    in_specs = [pl.BlockSpec((1, rows, d), lambda i, j: (i, j, 0))] + [full(w) for w in wts] + [tab_spec] * 3
    hs = lambda nh: pl.BlockSpec((1, nh, rows, LANES), lambda i, j: (i, 0, j, 0))
    out_shape = (
        jax.ShapeDtypeStruct((b, MLA_HEADS, s, LANES), BF16),
        jax.ShapeDtypeStruct((b, MLA_HEADS, s, LANES), BF16),
        jax.ShapeDtypeStruct((b, MLA_HEADS, s, LANES), BF16),
        jax.ShapeDtypeStruct((b, DIFF_HEADS, 2, s, LANES), BF16),
        jax.ShapeDtypeStruct((b, DIFF_HEADS, s, LANES), BF16),
        jax.ShapeDtypeStruct((b, DIFF_HEADS, s, 2 * LANES), BF16),
    )
    out_specs = (hs(MLA_HEADS), hs(MLA_HEADS), hs(MLA_HEADS),
                 pl.BlockSpec((1, DIFF_HEADS, 2, rows, LANES), lambda i, j: (i, 0, 0, j, 0)),
                 hs(DIFF_HEADS),
                 pl.BlockSpec((1, DIFF_HEADS, rows, 2 * LANES), lambda i, j: (i, 0, j, 0)))
    kern = functools.partial(_proj_kernel, scale_m=MLA_QK ** -0.5 * LOG2E, scale_d=DIFF_HD ** -0.5 * LOG2E)
    return pl.pallas_call(
        kern, grid=grid, in_specs=in_specs, out_specs=out_specs, out_shape=out_shape,
        compiler_params=pltpu.CompilerParams(dimension_semantics=("arbitrary", "arbitrary"),
                                             vmem_limit_bytes=VMEM_LIMIT),
        name="proj",
    )(x3, *wts, *tabs)


def _flash_kernel(*refs, hp, nm, tq, rc, diff, lambda_init):
    if diff:
        (q_ref, k_ref, v_ref, kmeta_ref, vmeta_ref, bmeta_ref, bdiag_ref, bprev_ref,
         lq1_ref, lk1_ref, lq2_ref, lk2_ref, subln_ref, o_ref, m_ref, acc_ref) = refs
    else:
        q_ref, k_ref, v_ref, kmeta_ref, vmeta_ref, bmeta_ref, bdiag_ref, o_ref, m_ref, acc_ref = refs
        bprev_ref = None
    qi = pl.program_id(2)
    nchunk = tq // rc
    nv = acc_ref.shape[1]
    chains = [(hh, mm, c) for hh in range(hp) for mm in range(nm) for c in range(nchunk)]
    qs = [q_ref[0, hh, mm, c * rc:(c + 1) * rc, :] for (hh, mm, c) in chains]

    def update(i, s, vt, first=False):
        rows = slice(i * rc, (i + 1) * rc)
        m_cur = jnp.max(s, axis=-1, keepdims=True)
        if first:
            m_new = jnp.broadcast_to(m_cur, (rc, LANES))
        else:
            m_prev = m_ref[rows]
            m_new = jnp.maximum(m_prev, m_cur)
        p = jnp.exp2((s - _lane_tile(m_new, s.shape[1] // LANES)).astype(BF16))
        pv = _dot(p, vt)
        if first:
            acc_ref[rows] = pv
        else:
            alpha = jnp.exp2(m_prev - m_new)
            acc_ref[rows] = _lane_tile(alpha, nv // LANES) * acc_ref[rows] + pv
        m_ref[rows] = m_new

    meta_scores = [_dot_nt(qs[i], kmeta_ref[hh]) + bmeta_ref[0, 0, c * rc:(c + 1) * rc, :]
                   for i, (hh, mm, c) in enumerate(chains)]
    for i, (hh, mm, c) in enumerate(chains):
        update(i, meta_scores[i], vmeta_ref[hh], first=True)

    def step(j, bias_ref=None, causal=False):
        off = pl.multiple_of(j * tq, tq)
        widths = [min(tq, -(-(c + 1) * rc // LANES) * LANES) if causal else tq for (hh, mm, c) in chains]
        scores = []
        for i, (hh, mm, c) in enumerate(chains):
            s = _dot_nt(qs[i], k_ref[0, hh, pl.ds(off, widths[i]), :])
            if bias_ref is not None:
                s = s + bias_ref[0, c * rc:(c + 1) * rc, 0:widths[i]]
            scores.append(s)
        for i, (hh, mm, c) in enumerate(chains):
            update(i, scores[i], v_ref[0, hh, pl.ds(off, widths[i]), :])

    def plain(j, carry):
        step(j)
        return carry

    if diff:
        lax.fori_loop(0, jnp.maximum(qi - 1, 0), plain, 0)
        pl.when(qi > 0)(lambda: step(qi - 1, bprev_ref))
    else:
        lax.fori_loop(0, qi, plain, 0)
    step(qi, bdiag_ref, causal=True)

    def normalized(i):
        acc = acc_ref[i * rc:(i + 1) * rc]
        return acc[:, :LANES] / acc[:, nv - 1:nv]

    outs = {ch: normalized(i) for i, ch in enumerate(chains)}
    if diff:
        lam = (jnp.exp(jnp.sum(lq1_ref[...] * lk1_ref[...], axis=-1, keepdims=True))
               - jnp.exp(jnp.sum(lq2_ref[...] * lk2_ref[...], axis=-1, keepdims=True)) + lambda_init)
        for c in range(nchunk):
            d = outs[(0, 0, c)] - lam * outs[(0, 1, c)]
            y = _rms(d, subln_ref[...]) * (1.0 - lambda_init)
            o_ref[0, c * rc:(c + 1) * rc, :] = y.astype(BF16)
    else:
        lo = lax.broadcasted_iota(jnp.int32, (1, LANES), 1) < MLA_V
        for c in range(nchunk):
            pair = jnp.where(lo, outs[(0, 0, c)], pltpu.roll(outs[(1, 0, c)], MLA_V, 1))
            o_ref[0, c * rc:(c + 1) * rc, :] = pair.astype(BF16)


def _flash(q, k, v, kmeta, vmeta, bmeta, bdiag, bprev, extra, *, diff, tq, lambda_init):
    b, g, nm, s, _ = q.shape
    nv = v.shape[-1]
    hp = 1 if diff else 2
    ngrid = g // hp
    nq = s // tq
    rows = tq
    rc = min(ATTN_CHAIN_ROWS, tq)
    grid = (b, ngrid, nq)
    gsel = (lambda gi: gi) if diff else (lambda gi: 0)
    in_specs = [
        pl.BlockSpec((1, hp, nm, tq, LANES), lambda bi, gi, qi: (bi, gi, 0, qi, 0)),
        pl.BlockSpec((1, hp, s, LANES), lambda bi, gi, qi: (bi, gi, 0, 0)),
        pl.BlockSpec((1, hp, s, nv), lambda bi, gi, qi: (bi, gi, 0, 0)),
        pl.BlockSpec((hp, LANES, LANES), lambda bi, gi, qi: (gi, 0, 0)),
        pl.BlockSpec((hp, LANES, nv), lambda bi, gi, qi: (gi, 0, 0)),
        pl.BlockSpec((1, 1, rows, LANES), lambda bi, gi, qi: (jnp.minimum(qi, 1), gsel(gi), 0, 0)),
        pl.BlockSpec((1, rows, tq), lambda bi, gi, qi: (gsel(gi), 0, 0)),
    ]
    args = [q, k, v, kmeta, vmeta, bmeta, bdiag]
    if diff:
        in_specs.append(pl.BlockSpec((1, rows, tq), lambda bi, gi, qi: (gi, 0, 0)))
        args.append(bprev)
        for e in extra:
            in_specs.append(pl.BlockSpec(e.shape, lambda bi, gi, qi: (0, 0)))
            args.append(e)
    kern = functools.partial(_flash_kernel, hp=hp, nm=nm, tq=tq, rc=rc, diff=diff, lambda_init=lambda_init)
    return pl.pallas_call(
        kern, grid=grid, in_specs=in_specs,
        out_specs=pl.BlockSpec((1, tq, LANES), lambda bi, gi, qi: (bi, qi, gi)),
        out_shape=jax.ShapeDtypeStruct((b, s, ngrid * LANES), BF16),
        scratch_shapes=[pltpu.VMEM((hp * nm * tq, LANES), F32), pltpu.VMEM((hp * nm * tq, nv), F32)],
        compiler_params=pltpu.CompilerParams(dimension_semantics=("arbitrary",) * 3,
                                             vmem_limit_bytes=VMEM_LIMIT),
        name="flash_diff" if diff else "flash_mla",
    )(*args)


def _top_values(s, n, row_ref=None):
    vals = []
    for i in range(n):
        m = jnp.max(s, axis=0, keepdims=True)
        vals.append(m)
        if row_ref is not None and i < PEER_TOPK:
            row_ref[i:i + 1, :] = m
        if i + 1 < n:
            s = jnp.where(s == m, -jnp.inf, s)
    return vals


def _front_kernel(x_ref, ym_ref, yd_ref, woa_ref, wob_ref, g2_ref, wqt_ref, sk_ref,
                  h1_ref, n2t_ref, rho_ref, p_ref, r_ref, v1s_ref, v2s_ref, cand_ref):
    h1 = x_ref[...] + _dot(ym_ref[...], woa_ref[...]) + _dot(yd_ref[...], wob_ref[...])
    h1_ref[...] = h1
    n2t = _rms(h1, g2_ref[...]).T.astype(BF16)
    n2t_ref[...] = n2t
    qpt = _dot(wqt_ref[...], n2t).astype(BF16)
    k = PEER_TOPK
    for h in range(PEER_HEADS):
        s1 = _dot(sk_ref[2 * h], qpt[(2 * h) * PEER_SUB:(2 * h + 1) * PEER_SUB])
        s2 = _dot(sk_ref[2 * h + 1], qpt[(2 * h + 1) * PEER_SUB:(2 * h + 2) * PEER_SUB])
        v1 = _top_values(s1, k + 1, v1s_ref)
        v2 = _top_values(s2, k + 1, v2s_ref)
        v2lo = v2s_ref[0:SUBLANES, :]
        row = lax.broadcasted_iota(jnp.int32, v2lo.shape, 0)
        cand_ref[0:k, :] = v1[0] + v2s_ref[...]
        cand_ref[k:k + SUBLANES, :] = v1[1] + v2lo
        for a in range(2, SUBLANES):
            lo = k + (a - 1) * SUBLANES
            cand_ref[lo:lo + SUBLANES, :] = jnp.where(row < (k + 1) // (a + 1), v1[a] + v2lo, -jnp.inf)
        cand_ref[k + 7 * SUBLANES:k + 8 * SUBLANES, :] = v1s_ref[SUBLANES:k, :] + v2[0]
        c = _top_values(cand_ref[...], k + 1)
        c17 = jnp.maximum(c[k], jnp.maximum(v1[k] + v2[0], v1[0] + v2[k]))
        tau = 0.5 * (c[k - 1] + c17)
        z = jnp.exp(c[0] - c[0])
        for i in range(1, k):
            z = z + jnp.exp(c[i] - c[0])
        e2 = [jnp.exp(v2[l] - v2[0]) for l in range(k)]
        n_all = 4
        rho = jnp.full(s1.shape, jnp.inf, F32)
        for l in range(n_all):
            rho = jnp.where(s1 + v2[l] >= tau, e2[l], rho)
        for a in range(k // (n_all + 1)):
            rho_a = jnp.full(tau.shape, jnp.inf, F32)
            for l in range(k // (a + 1)):
                rho_a = jnp.where(v1[a] + v2[l] >= tau, e2[l], rho_a)
            rho = jnp.where(s1 == v1[a], rho_a, rho)
        p = jnp.exp(s1 - v1[0]) * (math.sqrt(0.5) / z)
        r = jnp.exp(s2 - v2[0])
        for c in range(s1.shape[1] // LANES):
            cs = slice(c * LANES, (c + 1) * LANES)
            rho_ref[h, c] = rho[:, cs]
            p_ref[h, c] = p[:, cs]
            r_ref[h, c] = r[:, cs]


def _front(x2, ym, yd, woa, wob, g2, wqt, sk, rows):
    t, d = x2.shape
    rows = min(rows, t)
    grid = (t // rows,)
    full = lambda a: pl.BlockSpec(a.shape, lambda i: (0,) * a.ndim)
    assert rows % LANES == 0
    gate = jax.ShapeDtypeStruct((PEER_HEADS, t // LANES, PEER_KEYS, LANES), F32)
    gate_spec = pl.BlockSpec((PEER_HEADS, rows // LANES, PEER_KEYS, LANES), lambda i: (0, i, 0, 0))
    return pl.pallas_call(
        _front_kernel, grid=grid,
        in_specs=[pl.BlockSpec((rows, d), lambda i: (i, 0)),
                  pl.BlockSpec((rows, ym.shape[1]), lambda i: (i, 0)),
                  pl.BlockSpec((rows, yd.shape[1]), lambda i: (i, 0)),
                  full(woa), full(wob), full(g2), full(wqt), full(sk)],
        out_specs=(pl.BlockSpec((rows, d), lambda i: (i, 0)),
                   pl.BlockSpec((d, rows), lambda i: (0, i)),
                   gate_spec, gate_spec, gate_spec),
        out_shape=(jax.ShapeDtypeStruct((t, d), F32), jax.ShapeDtypeStruct((d, t), BF16),
                   gate, gate, gate),
        scratch_shapes=[pltpu.VMEM((PEER_TOPK, rows), F32), pltpu.VMEM((PEER_TOPK, rows), F32),
                        pltpu.VMEM((PEER_TOPK + 8 * SUBLANES, rows), F32)],
        compiler_params=pltpu.CompilerParams(dimension_semantics=("arbitrary",),
                                             vmem_limit_bytes=VMEM_LIMIT),
        name="front",
    )(x2, ym, yd, woa, wob, g2, wqt, sk)


def _dense_kernel(n2t_ref, u_ref, vt_ref, rho_ref, p_ref, r_ref, h1_ref, o_ref,
                  acc_ref, w_ref, pre_ref, brho_ref, bp_ref, wc_ref, vtc_ref):
    e = pl.program_id(1)
    n_tok = n2t_ref.shape[1]
    n_exp = u_ref.shape[0]
    piece = DENSE_PIECE
    n_piece = n_exp // piece
    first_keys = piece // PEER_KEYS
    n_i = n_exp // PEER_KEYS
    n_tc = n_tok // LANES
    assert n_tc % 2 == 0
    n_half = n_tc // 2
    first, second = range(0, n_half), range(n_half, n_tc)

    def lanes_of(ref, tcs, rows=slice(None)):
        return jnp.concatenate([ref[tc, rows, :] for tc in tcs], axis=1)

    @pl.when(e == 0)
    def _():
        acc_ref[...] = jnp.zeros_like(acc_ref)
        wc_ref[...] = jnp.zeros_like(wc_ref)
        vtc_ref[...] = jnp.zeros_like(vtc_ref)

    for h in range(PEER_HEADS):
        for ii in range(n_i):
            for tc in range(n_tc):
                brho_ref[h * n_i + ii, tc] = jnp.broadcast_to(rho_ref[h, tc, ii:ii + 1, :], (SUBLANES, LANES))
                bp_ref[h * n_i + ii, tc] = jnp.broadcast_to(p_ref[h, tc, ii:ii + 1, :], (SUBLANES, LANES))

    def pre_matmuls(tcs):
        cols = slice(tcs[0] * LANES, (tcs[-1] + 1) * LANES)
        for pc in range(n_piece):
            rows = slice(pc * piece, (pc + 1) * piece)
            val = _dot(u_ref[rows, :], n2t_ref[:, cols])
            for j, tc in enumerate(tcs):
                pre_ref[tc, rows, :] = val[:, j * LANES:(j + 1) * LANES]

    pre_matmuls(first)
    acc_ref[1] += _dot(vtc_ref[...], lanes_of(wc_ref, range(n_half)))
    pre_matmuls(second)

    def gate(pc, tcs):
        i0 = pc * first_keys
        nvr = GATE_ROWS // SUBLANES
        for tc in tcs:
            for jt in range(PEER_KEYS // GATE_ROWS):
                js = slice(jt * GATE_ROWS, (jt + 1) * GATE_ROWS)
                g = [None] * first_keys
                for h in range(PEER_HEADS):
                    r = r_ref[h, tc, js, :].reshape(nvr, SUBLANES, LANES)
                    for k in range(first_keys):
                        rho = brho_ref[h * n_i + i0 + k, tc][None]
                        p = bp_ref[h * n_i + i0 + k, tc][None]
                        term = jnp.where(r >= rho, p * r, 0.0)
                        g[k] = term if g[k] is None else g[k] + term
                for k in range(first_keys):
                    row0 = (i0 + k) * PEER_KEYS + jt * GATE_ROWS
                    pre = pre_ref[tc, row0:row0 + GATE_ROWS, :]
                    act = pre * (1.0 + lax.erf(pre))
                    w_ref[tc, row0:row0 + GATE_ROWS, :] = (g[k].reshape(GATE_ROWS, LANES) * act).astype(BF16)

    for pc in range(n_piece):
        gate(pc, first)
    acc_ref[0] += _dot(vt_ref[...], lanes_of(w_ref, first))
    for pc in range(n_piece):
        gate(pc, second)
    for j, tc in enumerate(second):
        wc_ref[j] = w_ref[tc]
    vtc_ref[...] = vt_ref[...]

    @pl.when(e == pl.num_programs(1) - 1)
    def _():
        acc_b = acc_ref[1] + _dot(vt_ref[...], lanes_of(wc_ref, range(n_half)))
        acc = jnp.concatenate([acc_ref[0], acc_b], axis=1)
        o_ref[...] = h1_ref[...] + acc.T


def _dense(n2t, u, vt, rho, p, r, h1, tokens, experts):
    d, t = n2t.shape
    ne = u.shape[0]
    tokens = min(tokens, t)
    grid = (t // tokens, ne // experts)
    n_i = experts // PEER_KEYS
    n_tc = tokens // LANES
    return pl.pallas_call(
        _dense_kernel, grid=grid,
        in_specs=[pl.BlockSpec((d, tokens), lambda ti, ei: (0, ti)),
                  pl.BlockSpec((experts, d), lambda ti, ei: (ei, 0)),
                  pl.BlockSpec((d, experts), lambda ti, ei: (0, ei)),
                  pl.BlockSpec((PEER_HEADS, n_tc, n_i, LANES), lambda ti, ei: (0, ti, ei, 0)),
                  pl.BlockSpec((PEER_HEADS, n_tc, n_i, LANES), lambda ti, ei: (0, ti, ei, 0)),
                  pl.BlockSpec((PEER_HEADS, n_tc, PEER_KEYS, LANES), lambda ti, ei: (0, ti, 0, 0)),
                  pl.BlockSpec((tokens, d), lambda ti, ei: (ti, 0))],
        out_specs=pl.BlockSpec((tokens, d), lambda ti, ei: (ti, 0)),
        out_shape=jax.ShapeDtypeStruct((t, d), F32),
        scratch_shapes=[pltpu.VMEM((2, d, tokens // 2), F32),
                        pltpu.VMEM((n_tc, experts, LANES), BF16),
                        pltpu.VMEM((n_tc, experts, LANES), F32),
                        pltpu.VMEM((PEER_HEADS * n_i, n_tc, SUBLANES, LANES), F32),
                        pltpu.VMEM((PEER_HEADS * n_i, n_tc, SUBLANES, LANES), F32),
                        pltpu.VMEM((n_tc // 2, experts, LANES), BF16),
                        pltpu.VMEM((d, experts), BF16)],
        compiler_params=pltpu.CompilerParams(dimension_semantics=("arbitrary", "arbitrary"),
                                             vmem_limit_bytes=VMEM_LIMIT),
        name="dense",
    )(n2t, u, vt, rho, p, r, h1)


def _rope_tables(pos):
    half = MLA_ROPE // 2
    inv_freq = ROPE_THETA ** (-jnp.arange(half, dtype=F32) / half)
    ang = pos.astype(F32)[:, None] * inv_freq[None, :]
    cos, sin = jnp.cos(ang), jnp.sin(ang)
    n = pos.shape[0]
    one = jnp.ones((n, MLA_NOPE), F32)
    zero = jnp.zeros((n, MLA_NOPE), F32)
    tail1 = jnp.ones((n, LANES - MLA_QK), F32)
    tail0 = jnp.zeros((n, LANES - MLA_QK), F32)
    zh = jnp.zeros((n, half), F32)
    cos_t = jnp.concatenate([one, cos, cos, tail1], axis=1)
    sina = jnp.concatenate([zero, -sin, zh, tail0], axis=1)
    sinb = jnp.concatenate([zero, zh, sin, tail0], axis=1)
    return cos_t, sina, sinb


def _bucket_np(dist):
    n = np.maximum(dist, 0)
    max_exact = NUM_BUCKETS // 2
    nf = np.maximum(n, 1).astype(np.float32)
    large = max_exact + (np.log(nf / np.float32(max_exact)) / np.float32(math.log(MAX_DISTANCE / max_exact))
                         * np.float32(NUM_BUCKETS - max_exact)).astype(np.int32)
    large = np.minimum(large, NUM_BUCKETS - 1)
    return np.where(n < max_exact, n, large)


def _bias_tiles(rel_bias, tq):
    far = int(np.min(np.nonzero(_bucket_np(np.arange(4 * MAX_DISTANCE)) == NUM_BUCKETS - 1)[0]))
    assert np.all(_bucket_np(np.arange(far, 8 * MAX_DISTANCE)) == NUM_BUCKETS - 1)
    if tq < far:
        raise NotImplementedError("attention tile smaller than the relative-bias window")
    table = rel_bias.astype(F32)
    shifted = (table - table[NUM_BUCKETS - 1][None, :]) * LOG2E

    def toeplitz(n_rows, n_cols, off):
        period = n_rows + n_cols - 1
        dist = np.arange(period) - (n_cols - 1) + off
        vals = jnp.where((dist >= 0)[:, None], shifted[_bucket_np(np.maximum(dist, 0))], NEG)
        u = vals[::-1].T
        flat = jnp.tile(u, (1, n_rows + 1))[:, :n_rows * (period + 1)]
        return flat.reshape(-1, n_rows, period + 1)[:, ::-1, :n_cols]

    diag = toeplitz(tq, tq, 0)
    prev = toeplitz(tq, tq, tq)
    pad = jnp.asarray(_mask_tiles(tq)[0][:, 0])
    nh = rel_bias.shape[1]
    meta0 = jnp.where(pad[0][None] < 0.0, NEG, toeplitz(tq, LANES, N_META))
    meta_far = jnp.broadcast_to(pad[1][None], (nh, tq, LANES))
    return jnp.stack([meta0, meta_far]), diag, prev


def _mask_tiles(tq):
    r = np.arange(tq)[:, None]
    c = np.arange(tq)[None, :]
    diag = np.where(r >= c, 0.0, NEG).astype(np.float32)[None]
    cm = np.arange(LANES)[None, :]
    meta = np.broadcast_to(np.where(cm < N_META, 0.0, NEG).astype(np.float32), (tq, LANES))
    return np.stack([meta, meta])[:, None], diag


def _pad_lanes(a, width):
    return jnp.pad(a, [(0, 0)] * (a.ndim - 1) + [(0, width - a.shape[-1])])


def kernel(x, meta_tokens, rel_bias, attn_norm, w_in, mla_q_norm, mla_w_uq, mla_kv_norm, mla_w_ukv,
           mla_qk_norm_q, mla_qk_norm_k, diff_q_norm, diff_k_norm, diff_lambda_q1, diff_lambda_k1,
           diff_lambda_q2, diff_lambda_k2, diff_subln, w_out, ffn_norm, peer_w_query, peer_sub_keys,
           peer_u, peer_v):
    b, s, d = x.shape
    assert attn_norm.shape[0] == 1, "meta rows are only used as keys: single layer"
    lambda_init = 0.8 - 0.6 * math.exp(-0.3 * 0)
    tq = min(ATTN_TILE, s)
    assert s % tq == 0

    wi = w_in[0]
    s1, s2, s3 = MLA_Q_RANK, MLA_Q_RANK + MLA_KV_RANK, MLA_Q_RANK + MLA_KV_RANK + MLA_ROPE
    wkr = jnp.pad(wi[:, s2:s3], ((0, 0), (MLA_NOPE, LANES - MLA_QK)))
    w_in_arr = jnp.concatenate([wi[:, :s2], wkr, wi[:, s3:]], axis=1).astype(BF16)
    wuq = _pad_lanes(mla_w_uq[0].reshape(MLA_Q_RANK, MLA_HEADS, MLA_QK), LANES).reshape(MLA_Q_RANK, -1).astype(BF16)
    wukv = mla_w_ukv[0].reshape(MLA_KV_RANK, MLA_HEADS, MLA_NOPE + MLA_V)
    wk = _pad_lanes(wukv[..., :MLA_NOPE], LANES).reshape(MLA_KV_RANK, -1)
    wv = _pad_lanes(wukv[..., MLA_NOPE:], LANES).reshape(MLA_KV_RANK, -1)
    wkv = jnp.concatenate([wk, wv], axis=1).astype(BF16)
    row = lambda a: a.reshape(1, -1).astype(F32)
    gq = _pad_lanes(row(mla_qk_norm_q[0]), LANES)
    gk = _pad_lanes(row(mla_qk_norm_k[0]), LANES)
    gdq = jnp.tile(row(diff_q_norm[0]), (1, 2))
    gdk = jnp.tile(row(diff_k_norm[0]), (1, 2))
    wts = (row(attn_norm[0]), w_in_arr, row(mla_q_norm[0]), wuq, row(mla_kv_norm[0]), wkv, gq, gk, gdq, gdk)

    pos_real = jnp.arange(N_META, N_META + s, dtype=jnp.int32)
    qm, km, vm, qd, kd, vd = _project(x, _rope_tables(pos_real), wts, PROJ_ROWS)
    pos_meta = jnp.arange(N_META, dtype=jnp.int32)
    _, km_m, vm_m, _, kd_m, vd_m = _project(meta_tokens[None].astype(x.dtype), _rope_tables(pos_meta), wts, N_META)
    padk = lambda a: jnp.pad(a[0], ((0, 0), (0, LANES - N_META), (0, 0)))

    mmeta, mdiag = (jnp.asarray(a) for a in _mask_tiles(tq))
    y_mla = _flash(qm[:, :, None], km, vm, padk(km_m), padk(vm_m), mmeta, mdiag, None, (),
                   diff=False, tq=tq, lambda_init=lambda_init)
    bmeta, bdiag, bprev = _bias_tiles(rel_bias, tq)
    extra = (row(diff_lambda_q1[0]), row(diff_lambda_k1[0]), row(diff_lambda_q2[0]), row(diff_lambda_k2[0]),
             row(diff_subln[0]))
    y_diff = _flash(qd, kd, vd, padk(kd_m), padk(vd_m), bmeta, bdiag, bprev, extra,
                    diff=True, tq=tq, lambda_init=lambda_init)

    t = b * s
    wo = w_out[0].astype(BF16)
    n_mla = MLA_HEADS * MLA_V
    wqt = peer_w_query[0].T.astype(BF16)
    sk = peer_sub_keys[0].reshape(PEER_HEADS * 2, PEER_KEYS, PEER_SUB).astype(BF16)
    h1, n2t, g_rho, g_p, g_r = _front(x.reshape(t, d), y_mla.reshape(t, -1), y_diff.reshape(t, -1),
                                      wo[:n_mla], wo[n_mla:], row(ffn_norm[0]), wqt, sk, FRONT_ROWS)
    u_bf = (peer_u[0] * math.sqrt(0.5)).astype(BF16)
    vt_bf = peer_v[0].T.astype(BF16)
    out = _dense(n2t, u_bf, vt_bf, g_rho, g_p, g_r, h1, DENSE_TOKENS, DENSE_EXPERTS)
    return out.reshape(b, s, d)
```
